```python
import math, functools
import jax, jax.numpy as jnp
from jax import lax
import numpy as np

D_MODEL = 1024
BATCH = 2
SEQ = 8192
DEPTH = 2
DEC_BATCH = 32
DEC_SEQ = 4
PAST_LEN = 8192
PAGE_SIZE = 128

A_HEADS = 16
A_HEAD_DIM = 64
A_WIDTH = A_HEADS * A_HEAD_DIM
Q_BLOCK = 128
POOL_WINDOWS = (2, 4, 8, 16)
POOL_GROUPS = len(POOL_WINDOWS)
POOL_GROUP_DIM = 128
B_WIDTH = POOL_GROUPS * POOL_GROUP_DIM
POOL_BUF = max(POOL_WINDOWS) - 1
C_HEADS = 4
C_HEAD_DIM = 128
C_WIDTH = C_HEADS * C_HEAD_DIM
CONV_WIDTH = 4
CONV_BUF = CONV_WIDTH - 1
CHUNK = 64
N_BRANCH = 3
D_FF = ((8 * D_MODEL // 3 + 127) // 128) * 128
DN_ALPHA = (2 * DEPTH) ** 0.25
DN_BETA = (8 * DEPTH) ** -0.25
LN_EPS = 1e-5
RMS_EPS = 1e-6
L2_EPS = 1e-6
POOL_SPARE_NUM = 5
POOL_SPARE_DEN = 4
IN_SPLITS = (A_WIDTH, A_WIDTH, A_WIDTH, A_HEADS, B_WIDTH, 3 * C_WIDTH, C_HEADS, C_HEADS, C_WIDTH, N_BRANCH * D_MODEL)
IN_WIDTH = sum(IN_SPLITS)
IN_SPLIT_IDX = tuple(int(i) for i in np.cumsum(IN_SPLITS)[:-1])

kernel_name = 'hybrid_fox_pool_gdn_decode_step'

F32 = jnp.float32


def layer_norm(x, g, b):
    xf = x.astype(F32)
    mu = xf.mean(-1, keepdims=True)
    var = jnp.square(xf - mu).mean(-1, keepdims=True)
    return ((xf - mu) * lax.rsqrt(var + LN_EPS) * g.astype(F32) + b.astype(F32)).astype(x.dtype)


def swiglu(x, w_up, w_down):
    gate, up = jnp.split(x @ w_up, 2, axis=-1)
    return (jax.nn.silu(gate) * up) @ w_down


def l2norm(x):
    return x * lax.rsqrt(jnp.sum(x * x, -1, keepdims=True) + L2_EPS)


def fox_prompt(q, k, v, logf):
    B, T, H, Dh = q.shape
    c = jnp.cumsum(logf, axis=1)
    c_key = c.transpose(0, 2, 1)[:, :, None, :]
    nb = T // Q_BLOCK
    qb = q.reshape(B, nb, Q_BLOCK, H, Dh).transpose(1, 0, 2, 3, 4)
    cb = c.reshape(B, nb, Q_BLOCK, H).transpose(1, 0, 2, 3)
    kpos = jnp.arange(T)
    scale = Dh ** -0.5

    def block(args):
        i, qi, ci = args
        s = jnp.einsum('bqhd,bkhd->bhqk', qi, k, preferred_element_type=F32) * scale
        s = s + ci.transpose(0, 2, 1)[..., None] - c_key
        qpos = i * Q_BLOCK + jnp.arange(Q_BLOCK)
        s = jnp.where(kpos[None, :] <= qpos[:, None], s, -jnp.inf)
        p = jax.nn.softmax(s, axis=-1)
        return jnp.einsum('bhqk,bkhd->bqhd', p.astype(v.dtype), v)

    o = lax.map(block, (jnp.arange(nb), qb, cb))
    return o.transpose(1, 0, 2, 3, 4).reshape(B, T, H, Dh)


def fox_sample(q, k, v, logf, k_past, v_past, logf_past):
    Bd, S, H, Dh = q.shape
    P = k_past.shape[1]
    c = jnp.cumsum(jnp.concatenate([logf_past.astype(F32), logf], axis=1), axis=1)
    c_q = c[:, P:].transpose(0, 2, 1)[..., None]
    c_key = c.transpose(0, 2, 1)[:, :, None, :]
    scale = Dh ** -0.5
    s_past = jnp.einsum('bqhd,bkhd->bhqk', q, k_past, preferred_element_type=F32)
    s_new = jnp.einsum('bqhd,bkhd->bhqk', q, k, preferred_element_type=F32)
    s = jnp.concatenate([s_past, s_new], axis=-1) * scale + c_q - c_key
    mask = jnp.concatenate([jnp.ones((S, P), bool), jnp.tril(jnp.ones((S, S), bool))], axis=-1)
    p = jax.nn.softmax(jnp.where(mask, s, -jnp.inf), axis=-1).astype(v.dtype)
    return (jnp.einsum('bhqk,bkhd->bqhd', p[..., :P], v_past)
            + jnp.einsum('bhqk,bkhd->bqhd', p[..., P:], v))


def pool_mix(u, prefix, n_valid, pool_w, pool_scale):
    B, T, _ = u.shape
    full = jnp.concatenate([prefix.astype(u.dtype), u], axis=1)
    fullf = full.astype(F32)
    csum = jnp.concatenate([jnp.zeros((B, 1, B_WIDTH), F32), jnp.cumsum(fullf, axis=1)], axis=1)
    hi = csum[:, POOL_BUF + 1:POOL_BUF + 1 + T]
    uf = fullf[:, POOL_BUF:]
    groups = []
    for gi, w in enumerate(POOL_WINDOWS):
        sl = slice(gi * POOL_GROUP_DIM, (gi + 1) * POOL_GROUP_DIM)
        lo = csum[:, POOL_BUF + 1 - w:POOL_BUF + 1 - w + T, sl]
        cnt = jnp.minimum(n_valid + jnp.arange(1, T + 1), w).astype(F32)[None, :, None]
        groups.append((hi[..., sl] - lo) / cnt - uf[..., sl])
    d = jnp.stack(groups, axis=2).astype(u.dtype)
    y = jnp.einsum('btgc,gcd->btgd', d, pool_w).reshape(B, T, B_WIDTH) * pool_scale
    return y, full[:, -POOL_BUF:]


def gated_delta_rule(q, k, v, g, beta, S0):
    B, T, H, Dk = q.shape
    Dv = v.shape[-1]
    C = min(CHUNK, T)
    pad = (-T) % C
    if pad:
        padt = lambda a: jnp.pad(a, [(0, 0), (0, pad)] + [(0, 0)] * (a.ndim - 2))
        q, k, v, g, beta = padt(q), padt(k), padt(v), padt(g), padt(beta)
    N = (T + pad) // C
    chunk4 = lambda a: a.reshape(B, N, C, H, a.shape[-1]).transpose(1, 0, 3, 2, 4)
    chunk3 = lambda a: a.reshape(B, N, C, H).transpose(1, 0, 3, 2)
    q, k, v, g, beta = chunk4(q), chunk4(k), chunk4(v), chunk3(g), chunk3(beta)
    G = jnp.cumsum(g, axis=-1)
    tri_incl = jnp.tril(jnp.ones((C, C), bool))
    tri_strict = jnp.tril(jnp.ones((C, C), bool), -1)
    gamma = jnp.exp(jnp.where(tri_incl, G[..., :, None] - G[..., None, :], -jnp.inf))
    kk = jnp.einsum('nbhid,nbhjd->nbhij', k, k)
    A = jnp.eye(C, dtype=F32) + jnp.where(tri_strict, beta[..., :, None] * kk * gamma, 0.0)
    eG = jnp.exp(G)[..., None]
    u = lax.linalg.triangular_solve(A, beta[..., None] * v, left_side=True, lower=True, unit_diagonal=True)
    w = lax.linalg.triangular_solve(A, beta[..., None] * k * eG, left_side=True, lower=True, unit_diagonal=True)
    qk = jnp.einsum('nbhid,nbhjd->nbhij', q, k) * gamma
    q_dec = q * eG
    k_dec = k * jnp.exp(G[..., -1:] - G)[..., None]
    g_last = jnp.exp(G[..., -1])

    def step(S, xs):
        u_i, w_i, qk_i, qd_i, kd_i, gl_i = xs
        v_new = u_i - jnp.einsum('bhcd,bhde->bhce', w_i, S)
        o = jnp.einsum('bhcd,bhde->bhce', qd_i, S) + jnp.einsum('bhij,bhje->bhie', qk_i, v_new)
        S = S * gl_i[..., None, None] + jnp.einsum('bhcd,bhce->bhde', kd_i, v_new)
        return S, o

    S, o = lax.scan(step, S0, (u, w, qk, q_dec, k_dec, g_last))
    o = o.transpose(1, 0, 3, 2, 4).reshape(B, N * C, H, Dv)[:, :T]
    return o, S


def gdn_mixer(qkv_raw, a, b, z, conv_prefix, ssm_state, conv_w, a_log, dt_bias, norm_w):
    B, T, _ = qkv_raw.shape
    full = jnp.concatenate([conv_prefix.astype(qkv_raw.dtype), qkv_raw], axis=1)
    y = full[:, 0:T] * conv_w[0]
    for j in range(1, CONV_WIDTH):
        y = y + full[:, j:j + T] * conv_w[j]
    y = jax.nn.silu(y).astype(F32)
    q, k, v = jnp.split(y, 3, axis=-1)
    heads = lambda t: t.reshape(B, T, C_HEADS, C_HEAD_DIM)
    q = l2norm(heads(q)) * (C_HEAD_DIM ** -0.5)
    k = l2norm(heads(k))
    v = heads(v)
    beta = jax.nn.sigmoid(b.astype(F32))
    g = -jnp.exp(a_log.astype(F32)) * jax.nn.softplus(a.astype(F32) + dt_bias.astype(F32))
    o, S = gated_delta_rule(q, k, v, g, beta, ssm_state.astype(F32))
    o = o * lax.rsqrt(jnp.mean(o * o, -1, keepdims=True) + RMS_EPS) * norm_w.astype(F32)
    o = o * jax.nn.silu(heads(z.astype(F32)))
    return o.reshape(B, T, C_WIDTH).astype(qkv_raw.dtype), full[:, -CONV_BUF:], S


def token_mixing(h, p, attend, pool_prefix, pool_valid, conv_prefix, ssm_state):
    B, T, _ = h.shape
    proj = h @ p['w_in']
    qa, ka, va, fa, ub, qkvc, ac, bc, zc, gl = jnp.split(proj, IN_SPLIT_IDX, axis=-1)
    heads = lambda t: t.reshape(B, T, A_HEADS, A_HEAD_DIM)
    qa, ka, va = heads(qa), heads(ka), heads(va)
    logf = jax.nn.log_sigmoid((fa + p['fox_f_bias']).astype(F32))
    o_a = attend(qa, ka, va, logf).reshape(B, T, A_WIDTH)
    o_b, pool_new = pool_mix(ub, pool_prefix, pool_valid, p['pool_w'], p['pool_scale'])
    o_c, conv_new, ssm_new = gdn_mixer(qkvc, ac, bc, zc, conv_prefix, ssm_state, p['gdn_conv_w'],
                                       p['gdn_a_log'], p['gdn_dt_bias'], p['gdn_norm_w'])
    gates = jax.nn.sigmoid(gl.astype(F32)).astype(h.dtype).reshape(B, T, N_BRANCH, D_MODEL)
    merged = (gates[:, :, 0] * (o_a @ p['w_branch_a'])
              + gates[:, :, 1] * (o_b @ p['w_branch_b'])
              + gates[:, :, 2] * (o_c @ p['w_branch_c']))
    return merged @ p['w_out'], (ka, va, logf, pool_new, conv_new, ssm_new)


def layer(x, p, attend, pool_prefix, pool_valid, conv_prefix, ssm_state):
    x = layer_norm(DN_ALPHA * x + 0.5 * swiglu(x, p['ffn_w_in'][0], p['ffn_w_out'][0]), p['ln_g'][0], p['ln_b'][0])
    m, st = token_mixing(x, p, attend, pool_prefix, pool_valid, conv_prefix, ssm_state)
    x = layer_norm(DN_ALPHA * x + m, p['ln_g'][1], p['ln_b'][1])
    x = layer_norm(DN_ALPHA * x + 0.5 * swiglu(x, p['ffn_w_in'][1], p['ffn_w_out'][1]), p['ln_g'][2], p['ln_b'][2])
    return x, st


def setup_inputs(seed: int = 0) -> dict:
    key = jax.random.key(seed)
    ks = jax.random.split(key, 26)
    nrm = lambda k, shape, s: jax.random.normal(k, shape, F32) * s
    n_pages = PAST_LEN // PAGE_SIZE
    n_used = DEC_BATCH * n_pages
    n_pool = (n_used * POOL_SPARE_NUM + POOL_SPARE_DEN - 1) // POOL_SPARE_DEN
    page_table = jax.random.permutation(ks[0], n_pool)[:n_used].reshape(DEC_BATCH, n_pages).astype(jnp.int32)
    x_prompt = nrm(ks[1], (BATCH, SEQ, D_MODEL), 1.0)
    x_sample = nrm(ks[2], (DEC_BATCH, DEC_SEQ, D_MODEL), 1.0)
    cache_k = nrm(ks[3], (DEPTH, n_pool, PAGE_SIZE, A_HEADS, A_HEAD_DIM), 1.0)
    cache_v = nrm(ks[4], (DEPTH, n_pool, PAGE_SIZE, A_HEADS, A_HEAD_DIM), 1.0)
    cache_logf = jax.nn.log_sigmoid(jax.random.uniform(ks[5], (DEPTH, n_pool, PAGE_SIZE, A_HEADS), F32, 1.0, 6.0))
    state_pool = nrm(ks[6], (DEPTH, DEC_BATCH, POOL_BUF, B_WIDTH), 1.0)
    state_conv = nrm(ks[7], (DEPTH, DEC_BATCH, CONV_BUF, 3 * C_WIDTH), 1.0)
    state_ssm = nrm(ks[8], (DEPTH, DEC_BATCH, C_HEADS, C_HEAD_DIM, C_HEAD_DIM), C_HEAD_DIM ** -0.5)
    w_in = nrm(ks[9], (DEPTH, D_MODEL, IN_WIDTH), D_MODEL ** -0.5)
    fox_f_bias = jax.random.uniform(ks[10], (DEPTH, A_HEADS), F32, 1.0, 4.0)
    gdn_conv_w = nrm(ks[11], (DEPTH, CONV_WIDTH, 3 * C_WIDTH), CONV_WIDTH ** -0.5)
    gdn_a_log = jnp.log(jax.random.uniform(ks[12], (DEPTH, C_HEADS), F32, 1.0, 16.0))
    dt = jnp.exp(jax.random.uniform(ks[13], (DEPTH, C_HEADS), F32, math.log(1e-3), math.log(1e-1)))
    gdn_dt_bias = dt + jnp.log(-jnp.expm1(-dt))
    gdn_norm_w = 1.0 + nrm(ks[14], (DEPTH, C_HEAD_DIM), 0.02)
    pool_w = nrm(ks[15], (DEPTH, POOL_GROUPS, POOL_GROUP_DIM, POOL_GROUP_DIM), POOL_GROUP_DIM ** -0.5)
    pool_scale = 1.0 + nrm(ks[16], (DEPTH, B_WIDTH), 0.02)
    w_branch_a = nrm(ks[17], (DEPTH, A_WIDTH, D_MODEL), A_WIDTH ** -0.5)
    w_branch_b = nrm(ks[18], (DEPTH, B_WIDTH, D_MODEL), B_WIDTH ** -0.5)
    w_branch_c = nrm(ks[19], (DEPTH, C_WIDTH, D_MODEL), C_WIDTH ** -0.5)
    w_out = nrm(ks[20], (DEPTH, D_MODEL, D_MODEL), DN_BETA * D_MODEL ** -0.5)
    ffn_w_in = nrm(ks[21], (DEPTH, 2, D_MODEL, 2 * D_FF), D_MODEL ** -0.5)
    ffn_w_out = nrm(ks[22], (DEPTH, 2, D_FF, D_MODEL), DN_BETA * D_FF ** -0.5)
    ln_g = 1.0 + nrm(ks[23], (DEPTH, 3, D_MODEL), 0.02)
    ln_b = nrm(ks[24], (DEPTH, 3, D_MODEL), 0.02)
    return {'x_prompt': x_prompt, 'x_sample': x_sample, 'cache_k': cache_k, 'cache_v': cache_v,
            'cache_logf': cache_logf, 'page_table': page_table, 'state_pool': state_pool,
            'state_conv': state_conv, 'state_ssm': state_ssm, 'w_in': w_in, 'fox_f_bias': fox_f_bias,
            'gdn_conv_w': gdn_conv_w, 'gdn_a_log': gdn_a_log, 'gdn_dt_bias': gdn_dt_bias,
            'gdn_norm_w': gdn_norm_w, 'pool_w': pool_w, 'pool_scale': pool_scale,
            'w_branch_a': w_branch_a, 'w_branch_b': w_branch_b, 'w_branch_c': w_branch_c,
            'w_out': w_out, 'ffn_w_in': ffn_w_in, 'ffn_w_out': ffn_w_out, 'ln_g': ln_g, 'ln_b': ln_b}


def reference(x_prompt, x_sample, cache_k, cache_v, cache_logf, page_table, state_pool, state_conv,
              state_ssm, w_in, fox_f_bias, gdn_conv_w, gdn_a_log, gdn_dt_bias, gdn_norm_w, pool_w,
              pool_scale, w_branch_a, w_branch_b, w_branch_c, w_out, ffn_w_in, ffn_w_out, ln_g, ln_b):
    yp, ys = x_prompt, x_sample
    bp = x_prompt.shape[0]
    bd = x_sample.shape[0]
    past = page_table.shape[1] * PAGE_SIZE
    outs_p = ([], [], [], [], [], [])
    outs_s = ([], [], [], [], [], [])
    for l in range(DEPTH):
        p = {'w_in': w_in[l], 'fox_f_bias': fox_f_bias[l], 'gdn_conv_w': gdn_conv_w[l],
             'gdn_a_log': gdn_a_log[l], 'gdn_dt_bias': gdn_dt_bias[l], 'gdn_norm_w': gdn_norm_w[l],
             'pool_w': pool_w[l], 'pool_scale': pool_scale[l], 'w_branch_a': w_branch_a[l],
             'w_branch_b': w_branch_b[l], 'w_branch_c': w_branch_c[l], 'w_out': w_out[l],
             'ffn_w_in': ffn_w_in[l], 'ffn_w_out': ffn_w_out[l], 'ln_g': ln_g[l], 'ln_b': ln_b[l]}
        yp, st_p = layer(yp, p, fox_prompt,
                         jnp.zeros((bp, POOL_BUF, B_WIDTH), x_prompt.dtype), 0,
                         jnp.zeros((bp, CONV_BUF, 3 * C_WIDTH), x_prompt.dtype),
                         jnp.zeros((bp, C_HEADS, C_HEAD_DIM, C_HEAD_DIM), F32))
        k_past = cache_k[l, page_table].reshape(bd, past, A_HEADS, A_HEAD_DIM)
        v_past = cache_v[l, page_table].reshape(bd, past, A_HEADS, A_HEAD_DIM)
        lf_past = cache_logf[l, page_table].reshape(bd, past, A_HEADS)
        attend = functools.partial(fox_sample, k_past=k_past, v_past=v_past, logf_past=lf_past)
        ys, st_s = layer(ys, p, attend, state_pool[l], min(POOL_BUF, past), state_conv[l], state_ssm[l])
        for lst, a in zip(outs_p, st_p):
            lst.append(a)
        for lst, a in zip(outs_s, st_s):
            lst.append(a)
    k_p, v_p, lf_p, pool_p, conv_p, ssm_p = [jnp.stack(a) for a in outs_p]
    k_s, v_s, lf_s, pool_s, conv_s, ssm_s = [jnp.stack(a) for a in outs_s]
    return (yp, ys, k_p, v_p, lf_p, pool_p, conv_p, ssm_p, k_s, v_s, lf_s, pool_s, conv_s, ssm_s)
```

```python
import functools

import jax
import jax.numpy as jnp
from jax import lax
from jax.experimental import pallas as pl
from jax.experimental.pallas import tpu as pltpu

F32 = jnp.float32
BF16 = jnp.bfloat16
HIGHEST = lax.Precision.HIGHEST

D_MODEL = 1024
DEPTH = 2
PAGE_SIZE = 128
A_HEADS = 16
A_HEAD_DIM = 64
A_WIDTH = A_HEADS * A_HEAD_DIM
POOL_WINDOWS = (2, 4, 8, 16)
POOL_GROUP_DIM = 128
B_WIDTH = len(POOL_WINDOWS) * POOL_GROUP_DIM
POOL_BUF = max(POOL_WINDOWS) - 1
C_HEADS = 4
C_HEAD_DIM = 128
C_WIDTH = C_HEADS * C_HEAD_DIM
CONV_WIDTH = 4
CONV_BUF = CONV_WIDTH - 1
N_BRANCH = 3
D_FF = ((8 * D_MODEL // 3 + 127) // 128) * 128
DN_ALPHA = (2 * DEPTH) ** 0.25
LN_EPS = 1e-5
RMS_EPS = 1e-6
L2_EPS = 1e-6
NEG_BIG = -1e30

VMEM_LIMIT_BYTES = 56 * 1024 * 1024
LANES = 128


def _params(*sem):
    return pltpu.CompilerParams(dimension_semantics=sem, vmem_limit_bytes=VMEM_LIMIT_BYTES)


def _layer_norm(y, g, b):
    mu = jnp.mean(y, axis=-1, keepdims=True)
    d = y - mu
    var = jnp.mean(d * d, axis=-1, keepdims=True)
    return d * lax.rsqrt(var + LN_EPS) * g + b


def _log_sigmoid(x):
    return jnp.minimum(x, 0.0) - jnp.log(1.0 + jnp.exp(-jnp.abs(x)))


def _softplus(x):
    return jnp.maximum(x, 0.0) + jnp.log(1.0 + jnp.exp(-jnp.abs(x)))


def _sigmoid(x):
    return 1.0 / (1.0 + jnp.exp(-x))


def _silu(x):
    return x * _sigmoid(x)


def _dot(a, b):
    return jnp.dot(a, b, preferred_element_type=F32)


def _dot_nt(a, b, precision=None):
    return lax.dot_general(a, b, (((1,), (1,)), ((), ())), preferred_element_type=F32, precision=precision)


def _dot_hi(a, b):
    return jnp.dot(a, b, preferred_element_type=F32, precision=HIGHEST)


def _ffn_ln_kernel(x_ref, wg_ref, wu_ref, wd_ref, g_ref, b_ref, o_ref, acc_ref):
    j = pl.program_id(1)

    @pl.when(j == 0)
    def _():
        acc_ref[...] = jnp.zeros_like(acc_ref)

    xb = x_ref[...].astype(BF16)
    gate = _dot(xb, wg_ref[...])
    up = _dot(xb, wu_ref[...])
    act = (_silu(gate) * up).astype(BF16)
    acc_ref[...] += _dot(act, wd_ref[...])

    @pl.when(j == pl.num_programs(1) - 1)
    def _():
        y = DN_ALPHA * x_ref[...] + 0.5 * acc_ref[...]
        o_ref[...] = _layer_norm(y, g_ref[...], b_ref[...])


def ffn_ln(x, w_up, w_down, g, b, *, tm, n_ff=2):
    m = x.shape[0]
    tf = D_FF // n_ff
    return pl.pallas_call(
        _ffn_ln_kernel,
        grid=(m // tm, n_ff),
        in_specs=[
            pl.BlockSpec((tm, D_MODEL), lambda i, j: (i, 0)),
            pl.BlockSpec((D_MODEL, tf), lambda i, j: (0, j)),
            pl.BlockSpec((D_MODEL, tf), lambda i, j: (0, j + n_ff)),
            pl.BlockSpec((tf, D_MODEL), lambda i, j: (j, 0)),
            pl.BlockSpec((1, D_MODEL), lambda i, j: (0, 0)),
            pl.BlockSpec((1, D_MODEL), lambda i, j: (0, 0)),
        ],
        out_specs=pl.BlockSpec((tm, D_MODEL), lambda i, j: (i, 0)),
        out_shape=jax.ShapeDtypeStruct((m, D_MODEL), F32),
        scratch_shapes=[pltpu.VMEM((tm, D_MODEL), F32)],
        compiler_params=_params("parallel", "arbitrary"),
        name="ffn_ln",
    )(x, w_up, w_up, w_down, g, b)


def _inproj_prompt_kernel(h_ref, wq_ref, wkt_ref, wvt_ref, wft_ref, fb_ref, wrest_ref,
                          q_ref, kt_ref, ktb_ref, vt_ref, vtb_ref, lft_ref, rest_ref):
    hb = h_ref[...].astype(BF16)
    q_ref[0] = _dot(hb, wq_ref[...]).astype(BF16)
    kt = _dot_nt(wkt_ref[...], hb)
    kt_ref[0] = kt
    ktb_ref[0] = kt.astype(BF16)
    vt = _dot_nt(wvt_ref[...], hb)
    vt_ref[0] = vt
    vtb_ref[0] = vt.astype(BF16)
    lft_ref[0] = _log_sigmoid(_dot_nt(wft_ref[...], hb) + fb_ref[...])
    rest_ref[...] = _dot(hb, wrest_ref[...])


def inproj_prompt(h, wq, wkt, wvt, wft, fbias, wrest, *, batch, seq, tm):
    m = h.shape[0]
    nt = seq // tm
    n_rest = wrest.shape[1]
    const = lambda i: (0, 0)
    tok3 = lambda i: (i // nt, i % nt, 0)
    feat3 = lambda i: (i // nt, 0, i % nt)
    return pl.pallas_call(
        _inproj_prompt_kernel,
        grid=(m // tm,),
        in_specs=[
            pl.BlockSpec((tm, D_MODEL), lambda i: (i, 0)),
            pl.BlockSpec((D_MODEL, A_WIDTH), const),
            pl.BlockSpec((A_WIDTH, D_MODEL), const),
            pl.BlockSpec((A_WIDTH, D_MODEL), const),
            pl.BlockSpec((A_HEADS, D_MODEL), const),
            pl.BlockSpec((A_HEADS, 1), const),
            pl.BlockSpec((D_MODEL, n_rest), const),
        ],
        out_specs=[
            pl.BlockSpec((1, tm, A_WIDTH), tok3),
            pl.BlockSpec((1, A_WIDTH, tm), feat3),
            pl.BlockSpec((1, A_WIDTH, tm), feat3),
            pl.BlockSpec((1, A_WIDTH, tm), feat3),
            pl.BlockSpec((1, A_WIDTH, tm), feat3),
            pl.BlockSpec((1, A_HEADS, tm), feat3),
            pl.BlockSpec((tm, n_rest), lambda i: (i, 0)),
        ],
        out_shape=[
            jax.ShapeDtypeStruct((batch, seq, A_WIDTH), BF16),
            jax.ShapeDtypeStruct((batch, A_WIDTH, seq), F32),
            jax.ShapeDtypeStruct((batch, A_WIDTH, seq), BF16),
            jax.ShapeDtypeStruct((batch, A_WIDTH, seq), F32),
            jax.ShapeDtypeStruct((batch, A_WIDTH, seq), BF16),
            jax.ShapeDtypeStruct((batch, A_HEADS, seq), F32),
            jax.ShapeDtypeStruct((m, n_rest), F32),
        ],
        compiler_params=_params("parallel"),
        name="inproj_prompt",
    )(h, wq, wkt, wvt, wft, fbias, wrest)


def _matmul_kernel(x_ref, w_ref, o_ref):
    o_ref[...] = _dot(x_ref[...].astype(BF16), w_ref[...])


def matmul_rows(x, w, *, tm):
    m, k = x.shape
    n = w.shape[1]
    return pl.pallas_call(
        _matmul_kernel,
        grid=(m // tm,),
        in_specs=[pl.BlockSpec((tm, k), lambda i: (i, 0)), pl.BlockSpec((k, n), lambda i: (0, 0))],
        out_specs=pl.BlockSpec((tm, n), lambda i: (i, 0)),
        out_shape=jax.ShapeDtypeStruct((m, n), F32),
        compiler_params=_params("parallel"),
        name="matmul_rows",
    )(x, w)


def _cumsum_lanes_kernel(x_ref, o_ref, carry_ref, *, tb):
    @pl.when(pl.program_id(1) == 0)
    def _():
        carry_ref[...] = jnp.zeros_like(carry_ref)

    r = lax.broadcasted_iota(jnp.int32, (tb, tb), 0)
    c = lax.broadcasted_iota(jnp.int32, (tb, tb), 1)
    upper = (r <= c).astype(F32)
    out = _dot_hi(x_ref[0], upper) + carry_ref[...]
    o_ref[0] = out
    carry_ref[...] = out[:, tb - 1:tb]


def cumsum_lanes(x, *, tb):
    b, h, t = x.shape
    return pl.pallas_call(
        functools.partial(_cumsum_lanes_kernel, tb=tb),
        grid=(b, t // tb),
        in_specs=[pl.BlockSpec((1, h, tb), lambda i, j: (i, 0, j))],
        out_specs=pl.BlockSpec((1, h, tb), lambda i, j: (i, 0, j)),
        out_shape=jax.ShapeDtypeStruct((b, h, t), F32),
        scratch_shapes=[pltpu.VMEM((h, 1), F32)],
        compiler_params=_params("parallel", "arbitrary"),
        name="cumsum_lanes",
    )(x)


def _fox_prompt_kernel(q_ref, kt_ref, vt_ref, c_ref, o_ref, *, tq):
    i = pl.program_id(2)
    q = q_ref[0]
    lane = lax.broadcasted_iota(jnp.int32, (tq, LANES), 1)
    low = lane < A_HEAD_DIM
    zero = jnp.zeros_like(q)
    row = lax.broadcasted_iota(jnp.int32, (tq, tq), 0)
    col = lax.broadcasted_iota(jnp.int32, (tq, tq), 1)
    causal = col <= row

    def head(qh, hh):
        def step(j, carry, masked):
            m, l, acc = carry
            off = pl.multiple_of(j * tq, tq)
            s = _dot(qh, kt_ref[0, :, pl.ds(off, tq)]) - c_ref[0, 0, hh:hh + 1, pl.ds(off, tq)]
            if masked:
                s = jnp.where(causal, s, NEG_BIG)
            m_new = jnp.maximum(m, jnp.max(s, axis=-1, keepdims=True))
            alpha = jnp.exp(m - m_new)
            p = jnp.exp(s - m_new)
            l = alpha * l + jnp.sum(p, axis=-1, keepdims=True)
            acc = alpha * acc + _dot_nt(p.astype(BF16), vt_ref[0, :, pl.ds(off, tq)])
            return m_new, l, acc

        init = (jnp.full((tq, 1), NEG_BIG, F32), jnp.zeros((tq, 1), F32), jnp.zeros((tq, LANES), F32))
        carry = lax.fori_loop(0, i, lambda j, c: step(j, c, False), init)
        m, l, acc = step(i, carry, True)
        return acc / l

    o0 = head(jnp.where(low, q, zero), 0)
    o1 = head(jnp.where(low, zero, q), 1)
    o_ref[0] = jnp.where(low, o0, o1).astype(o_ref.dtype)


def fox_prompt(q, kt, vt, c, *, tq):
    b, t, _ = q.shape
    pairs = A_HEADS // 2
    c4 = c.reshape(b, pairs, 2, t)
    return pl.pallas_call(
        functools.partial(_fox_prompt_kernel, tq=tq),
        grid=(b, pairs, t // tq),
        in_specs=[
            pl.BlockSpec((1, tq, LANES), lambda bi, hp, i: (bi, i, hp)),
            pl.BlockSpec((1, LANES, t), lambda bi, hp, i: (bi, hp, 0)),
            pl.BlockSpec((1, LANES, t), lambda bi, hp, i: (bi, hp, 0)),
            pl.BlockSpec((1, 1, 2, t), lambda bi, hp, i: (bi, hp, 0, 0)),
        ],
        out_specs=pl.BlockSpec((1, tq, LANES), lambda bi, hp, i: (bi, i, hp)),
        out_shape=jax.ShapeDtypeStruct((b, t, A_WIDTH), BF16),
        compiler_params=_params("parallel", "parallel", "arbitrary"),
        name="fox_prompt",
    )(q, kt, vt, c4)


def _fox_sample_kernel(pt_ref, q_ref, kn_ref, vn_ref, lfn_ref, kp_ref, vp_ref, lfp_ref, o_ref,
                       qbd_ref, m_ref, l_ref, acc_ref, carry_ref, *, n_q):
    p = pl.program_id(1)
    rows = n_q * A_HEADS
    hrow = lax.broadcasted_iota(jnp.int32, (A_HEADS, A_WIDTH), 0)
    hcol = lax.broadcasted_iota(jnp.int32, (A_HEADS, A_WIDTH), 1) // A_HEAD_DIM
    head_mask = hrow == hcol

    @pl.when(p == 0)
    def _():
        m_ref[...] = jnp.full_like(m_ref, NEG_BIG)
        l_ref[...] = jnp.zeros_like(l_ref)
        acc_ref[...] = jnp.zeros_like(acc_ref)
        carry_ref[...] = jnp.zeros_like(carry_ref)
        for qi in range(n_q):
            qrow = jnp.broadcast_to(q_ref[0, qi:qi + 1, :], (A_HEADS, A_WIDTH))
            qbd_ref[qi * A_HEADS:(qi + 1) * A_HEADS, :] = jnp.where(head_mask, qrow, 0.0)

    s = _dot(qbd_ref[...].astype(BF16), kp_ref[...].astype(BF16))
    lf = lfp_ref[...]
    r = lax.broadcasted_iota(jnp.int32, (PAGE_SIZE, PAGE_SIZE), 0)
    c = lax.broadcasted_iota(jnp.int32, (PAGE_SIZE, PAGE_SIZE), 1)
    after = (r > c).astype(F32)
    bias = _dot_hi(lf, after) + carry_ref[...]
    carry_ref[...] += jnp.sum(lf, axis=-1, keepdims=True)
    s = s + jnp.concatenate([bias] * n_q, axis=0)
    m_old = m_ref[...]
    m_new = jnp.maximum(m_old, jnp.max(s, axis=-1, keepdims=True))
    alpha = jnp.exp(m_old - m_new)
    pr = jnp.exp(s - m_new)
    l_ref[...] = alpha * l_ref[...] + jnp.sum(pr, axis=-1, keepdims=True)
    acc_ref[...] = alpha * acc_ref[...] + _dot_nt(pr.astype(BF16), vp_ref[...].astype(BF16))
    m_ref[...] = m_new

    @pl.when(p == pl.num_programs(1) - 1)
    def _():
        qbd = qbd_ref[...]
        lfn = lfn_ref[0]
        qidx = lax.broadcasted_iota(jnp.int32, (rows, 1), 0) // A_HEADS
        s_new = []
        cum = jnp.zeros((A_HEADS, 1), F32)
        for ki in range(n_q):
            cum = cum + lfn[:, ki:ki + 1]
            sk = jnp.sum(qbd * kn_ref[0, ki:ki + 1, :], axis=-1, keepdims=True)
            sk = sk - jnp.concatenate([cum] * n_q, axis=0)
            s_new.append(jnp.where(qidx >= ki, sk, NEG_BIG))
        m_old = m_ref[...]
        m_fin = m_old
        for sk in s_new:
            m_fin = jnp.maximum(m_fin, sk)
        alpha = jnp.exp(m_old - m_fin)
        l_fin = alpha * l_ref[...]
        acc = alpha * acc_ref[...]
        for ki, sk in enumerate(s_new):
            pk = jnp.exp(sk - m_fin)
            l_fin = l_fin + pk
            acc = acc + pk * vn_ref[0, ki:ki + 1, :]
        acc = acc / l_fin
        for qi in range(n_q):
            blk = jnp.where(head_mask, acc[qi * A_HEADS:(qi + 1) * A_HEADS, :], 0.0)
            o_ref[0, qi:qi + 1, :] = jnp.sum(blk, axis=0, keepdims=True)


def fox_sample(page_table, q, k_new, v_new, lf_new_t, cache_kt, cache_vt, cache_lft, *, layer):
    bd, n_q, _ = q.shape
    n_pages = page_table.shape[1]
    rows = n_q * A_HEADS
    pt = page_table.reshape(-1)

    def page(b, p, pt_ref):
        return (layer, pt_ref[b * n_pages + (n_pages - 1 - p)], 0, 0)

    seq3 = lambda b, p, pt_ref: (b, 0, 0)
    grid_spec = pltpu.PrefetchScalarGridSpec(
        num_scalar_prefetch=1,
        grid=(bd, n_pages),
        in_specs=[
            pl.BlockSpec((1, n_q, A_WIDTH), seq3),
            pl.BlockSpec((1, n_q, A_WIDTH), seq3),
            pl.BlockSpec((1, n_q, A_WIDTH), seq3),
            pl.BlockSpec((1, A_HEADS, n_q), seq3),
            pl.BlockSpec((None, None, A_WIDTH, PAGE_SIZE), page),
            pl.BlockSpec((None, None, A_WIDTH, PAGE_SIZE), page),
            pl.BlockSpec((None, None, A_HEADS, PAGE_SIZE), page),
        ],
        out_specs=pl.BlockSpec((1, n_q, A_WIDTH), seq3),
        scratch_shapes=[
            pltpu.VMEM((rows, A_WIDTH), F32),
            pltpu.VMEM((rows, 1), F32),
            pltpu.VMEM((rows, 1), F32),
            pltpu.VMEM((rows, A_WIDTH), F32),
            pltpu.VMEM((A_HEADS, 1), F32),
        ],
    )
    return pl.pallas_call(
        functools.partial(_fox_sample_kernel, n_q=n_q),
        grid_spec=grid_spec,
        out_shape=jax.ShapeDtypeStruct((bd, n_q, A_WIDTH), F32),
        compiler_params=_params("parallel", "arbitrary"),
        name="fox_sample",
    )(pt, q, k_new, v_new, lf_new_t, cache_kt, cache_vt, cache_lft)


def _pool_kernel(u_ref, pre_ref, w_ref, sc_ref, o_ref, ext_ref, *, tt, n_valid):
    halo = POOL_BUF + 1
    j = pl.program_id(1)

    @pl.when(j == 0)
    def _():
        ext_ref[0:halo, :] = pre_ref[0]

    @pl.when(j > 0)
    def _():
        ext_ref[0:halo, :] = ext_ref[tt:tt + halo, :]

    u = u_ref[0]
    ext_ref[halo:halo + tt, :] = u
    pos = j * tt + lax.broadcasted_iota(jnp.int32, (tt, 1), 0) + (1 + n_valid)
    outs = []
    for gi, w in enumerate(POOL_WINDOWS):
        sl = slice(gi * POOL_GROUP_DIM, (gi + 1) * POOL_GROUP_DIM)
        tot = u[:, sl]
        for k in range(1, w):
            tot = tot + ext_ref[halo - k:halo - k + tt, sl]
        cnt = jnp.minimum(pos, w).astype(F32)
        d = tot / cnt - u[:, sl]
        outs.append(_dot(d.astype(BF16), w_ref[gi]))
    o_ref[0] = (jnp.concatenate(outs, axis=-1) * sc_ref[...]).astype(o_ref.dtype)


def pool_mix(u, prefix16, pool_w, pool_scale, *, tt, n_valid, out_dtype):
    b, t, _ = u.shape
    halo = POOL_BUF + 1
    return pl.pallas_call(
        functools.partial(_pool_kernel, tt=tt, n_valid=n_valid),
        grid=(b, t // tt),
        in_specs=[
            pl.BlockSpec((1, tt, B_WIDTH), lambda i, j: (i, j, 0)),
            pl.BlockSpec((1, halo, B_WIDTH), lambda i, j: (i, 0, 0)),
            pl.BlockSpec((len(POOL_WINDOWS), POOL_GROUP_DIM, POOL_GROUP_DIM), lambda i, j: (0, 0, 0)),
            pl.BlockSpec((1, B_WIDTH), lambda i, j: (0, 0)),
        ],
        out_specs=pl.BlockSpec((1, tt, B_WIDTH), lambda i, j: (i, j, 0)),
        out_shape=jax.ShapeDtypeStruct((b, t, B_WIDTH), out_dtype),
        scratch_shapes=[pltpu.VMEM((tt + 2 * halo, B_WIDTH), F32)],
        compiler_params=_params("parallel", "arbitrary"),
        name="pool_mix",
    )(u, prefix16, pool_w, pool_scale)


def _gdn_kernel(x_ref, ab_ref, z_ref, pre_ref, s0_ref, cw_ref, alog_ref, dtb_ref, nw_ref,
                o_ref, s_out_ref, ext_ref, s_ref, *, chunk, t_valid):
    j = pl.program_id(1)
    halo = 8

    @pl.when(j == 0)
    def _():
        ext_ref[0:halo, :] = pre_ref[0]
        s_ref[...] = s0_ref[0]

    @pl.when(j > 0)
    def _():
        ext_ref[0:halo, :] = ext_ref[chunk:chunk + halo, :]

    ext_ref[halo:halo + chunk, :] = x_ref[0]
    y = ext_ref[halo:halo + chunk, :] * cw_ref[CONV_WIDTH - 1:CONV_WIDTH, :]
    for k in range(1, CONV_WIDTH):
        y = y + ext_ref[halo - k:halo - k + chunk, :] * cw_ref[CONV_WIDTH - 1 - k:CONV_WIDTH - k, :]
    y = _silu(y)

    ab = ab_ref[0]
    valid = (j * chunk + lax.broadcasted_iota(jnp.int32, (chunk, 1), 0)) < t_valid
    g_all = jnp.where(valid, -jnp.exp(alog_ref[...]) * _softplus(ab + dtb_ref[...]), 0.0)
    beta_all = jnp.where(valid, _sigmoid(ab), 0.0)
    r = lax.broadcasted_iota(jnp.int32, (chunk, chunk), 0)
    c = lax.broadcasted_iota(jnp.int32, (chunk, chunk), 1)
    incl = r >= c
    strict = r > c
    eye = (r == c).astype(F32)
    gcum = _dot_hi(incl.astype(F32), g_all)
    eye_l = (lax.broadcasted_iota(jnp.int32, (LANES, LANES), 0)
             == lax.broadcasted_iota(jnp.int32, (LANES, LANES), 1)).astype(F32)
    gcum_t = _dot_nt(eye_l, gcum, precision=HIGHEST)
    z = z_ref[0]
    outs = []
    for h in range(C_HEADS):
        sl = slice(h * C_HEAD_DIM, (h + 1) * C_HEAD_DIM)
        q = y[:, sl]
        k = y[:, C_WIDTH + h * C_HEAD_DIM:C_WIDTH + (h + 1) * C_HEAD_DIM]
        v = y[:, 2 * C_WIDTH + h * C_HEAD_DIM:2 * C_WIDTH + (h + 1) * C_HEAD_DIM]
        q = q * lax.rsqrt(jnp.sum(q * q, axis=-1, keepdims=True) + L2_EPS) * (C_HEAD_DIM ** -0.5)
        k = k * lax.rsqrt(jnp.sum(k * k, axis=-1, keepdims=True) + L2_EPS)
        beta = beta_all[:, C_HEADS + h:C_HEADS + h + 1]
        gc = gcum[:, h:h + 1]
        gr = gcum_t[h:h + 1, :]
        gamma = jnp.exp(jnp.where(incl, gc - gr, -jnp.inf))
        eg = jnp.exp(gc)
        kk = _dot_nt(k, k, precision=HIGHEST)
        n = jnp.where(strict, beta * kk * gamma, 0.0)
        inv = eye - n
        pw = n
        steps = max(1, (chunk - 1).bit_length()) - 1
        for _ in range(steps):
            pw = _dot_hi(pw, pw)
            inv = inv + _dot_hi(inv, pw)
        u = _dot_hi(inv, beta * v)
        w = _dot_hi(inv, beta * k * eg)
        qk = _dot_nt(q, k, precision=HIGHEST) * gamma
        g_last = gcum[chunk - 1:chunk, h:h + 1]
        k_dec = k * jnp.exp(g_last - gc)
        s_old = s_ref[h]
        v_new = u - _dot_hi(w, s_old)
        o = _dot_hi(q * eg, s_old) + _dot_hi(qk, v_new)
        s_ref[h] = s_old * jnp.exp(g_last) + lax.dot_general(
            k_dec, v_new, (((0,), (0,)), ((), ())), preferred_element_type=F32, precision=HIGHEST)
        o = o * lax.rsqrt(jnp.mean(o * o, axis=-1, keepdims=True) + RMS_EPS) * nw_ref[...]
        outs.append(o * _silu(z[:, sl]))
    o_ref[0] = jnp.concatenate(outs, axis=-1).astype(o_ref.dtype)

    @pl.when(j == pl.num_programs(1) - 1)
    def _():
        s_out_ref[0] = s_ref[...]


def gdn_mixer(x, ab, z, prefix8, s0, conv_w, a_log_row, dt_bias_row, norm_w, *, chunk, t_valid, out_dtype):
    b, t, _ = x.shape
    row = lambda i, j: (0, 0)
    tok = lambda i, j: (i, j, 0)
    seq = lambda i, j: (i, 0, 0)
    return pl.pallas_call(
        functools.partial(_gdn_kernel, chunk=chunk, t_valid=t_valid),
        grid=(b, t // chunk),
        in_specs=[
            pl.BlockSpec((1, chunk, 3 * C_WIDTH), tok),
            pl.BlockSpec((1, chunk, LANES), tok),
            pl.BlockSpec((1, chunk, C_WIDTH), tok),
            pl.BlockSpec((1, 8, 3 * C_WIDTH), seq),
            pl.BlockSpec((1, C_HEADS, C_HEAD_DIM, C_HEAD_DIM), lambda i, j: (i, 0, 0, 0)),
            pl.BlockSpec((CONV_WIDTH, 3 * C_WIDTH), row),
            pl.BlockSpec((1, LANES), row),
            pl.BlockSpec((1, LANES), row),
            pl.BlockSpec((1, C_HEAD_DIM), row),
        ],
        out_specs=[
            pl.BlockSpec((1, chunk, C_WIDTH), tok),
            pl.BlockSpec((1, C_HEADS, C_HEAD_DIM, C_HEAD_DIM), lambda i, j: (i, 0, 0, 0)),
        ],
        out_shape=[
            jax.ShapeDtypeStruct((b, t, C_WIDTH), out_dtype),
            jax.ShapeDtypeStruct((b, C_HEADS, C_HEAD_DIM, C_HEAD_DIM), F32),
        ],
        scratch_shapes=[
            pltpu.VMEM((chunk + 16, 3 * C_WIDTH), F32),
            pltpu.VMEM((C_HEADS, C_HEAD_DIM, C_HEAD_DIM), F32),
        ],
        compiler_params=_params("parallel", "arbitrary"),
        name="gdn_mixer",
    )(x, ab, z, prefix8, s0, conv_w, a_log_row, dt_bias_row, norm_w)


def _merge_kernel(h_ref, oa_ref, ob_ref, oc_ref, wg_ref, wa_ref, wb_ref, wc_ref, wo_ref, g_ref, b_ref, o_ref):
    h = h_ref[...]
    hb = h.astype(BF16)
    merged = None
    for i, (x_ref, w_ref) in enumerate(((oa_ref, wa_ref), (ob_ref, wb_ref), (oc_ref, wc_ref))):
        gate = _sigmoid(_dot(hb, wg_ref[:, i * D_MODEL:(i + 1) * D_MODEL]))
        term = gate * _dot(x_ref[...].astype(BF16), w_ref[...])
        merged = term if merged is None else merged + term
    y = DN_ALPHA * h + _dot(merged.astype(BF16), wo_ref[...])
    o_ref[...] = _layer_norm(y, g_ref[...], b_ref[...])


def merge_ln(h, oa, ob, oc, wg, wa, wb, wc, wo, g, b, *, tm):
    m = h.shape[0]
    const = lambda i: (0, 0)
    rows = lambda i: (i, 0)
    return pl.pallas_call(
        _merge_kernel,
        grid=(m // tm,),
        in_specs=[
            pl.BlockSpec((tm, D_MODEL), rows),
            pl.BlockSpec((tm, A_WIDTH), rows),
            pl.BlockSpec((tm, B_WIDTH), rows),
            pl.BlockSpec((tm, C_WIDTH), rows),
            pl.BlockSpec((D_MODEL, N_BRANCH * D_MODEL), const),
            pl.BlockSpec((A_WIDTH, D_MODEL), const),
            pl.BlockSpec((B_WIDTH, D_MODEL), const),
            pl.BlockSpec((C_WIDTH, D_MODEL), const),
            pl.BlockSpec((D_MODEL, D_MODEL), const),
            pl.BlockSpec((1, D_MODEL), const),
            pl.BlockSpec((1, D_MODEL), const),
        ],
        out_specs=pl.BlockSpec((tm, D_MODEL), rows),
        out_shape=jax.ShapeDtypeStruct((m, D_MODEL), F32),
        compiler_params=_params("parallel"),
        name="merge_ln",
    )(h, oa, ob, oc, wg, wa, wb, wc, wo, g, b)


def _split_w_in(w_in_l):
    o = 0
    parts = {}
    for name, n in (("q", A_WIDTH), ("k", A_WIDTH), ("v", A_WIDTH), ("f", A_HEADS), ("ub", B_WIDTH),
                    ("qkvc", 3 * C_WIDTH), ("a", C_HEADS), ("b", C_HEADS), ("z", C_WIDTH),
                    ("gate", N_BRANCH * D_MODEL)):
        parts[name] = w_in_l[:, o:o + n]
        o += n
    return parts


def kernel(x_prompt, x_sample, cache_k, cache_v, cache_logf, page_table, state_pool, state_conv, state_ssm,
           w_in, fox_f_bias, gdn_conv_w, gdn_a_log, gdn_dt_bias, gdn_norm_w, pool_w, pool_scale,
           w_branch_a, w_branch_b, w_branch_c, w_out, ffn_w_in, ffn_w_out, ln_g, ln_b):
    bp, seq, _ = x_prompt.shape
    bd, n_q, _ = x_sample.shape
    mp, ms = bp * seq, bd * n_q
    tm_p = 512
    tm_s = ms

    n_pool = cache_k.shape[1]
    cache_kt = cache_k.transpose(0, 1, 3, 4, 2).reshape(DEPTH, n_pool, A_WIDTH, PAGE_SIZE)
    cache_vt = cache_v.transpose(0, 1, 3, 4, 2).reshape(DEPTH, n_pool, A_WIDTH, PAGE_SIZE)
    cache_lft = cache_logf.transpose(0, 1, 3, 2)

    yp = x_prompt.reshape(mp, D_MODEL)
    ys = x_sample.reshape(ms, D_MODEL)
    outs_p = ([], [], [], [], [], [])
    outs_s = ([], [], [], [], [], [])
    zeros_ab = jnp.zeros((D_MODEL, LANES - 2 * C_HEADS), F32)
    for l in range(DEPTH):
        wp = _split_w_in(w_in[l])
        wq = (wp["q"] * (A_HEAD_DIM ** -0.5)).astype(BF16)
        w_ab = jnp.concatenate([wp["a"], wp["b"], zeros_ab], axis=1)
        w_rest = jnp.concatenate([wp["qkvc"], wp["ub"], wp["z"], w_ab], axis=1).astype(BF16)
        o_qkvc, o_ub, o_z, o_ab = 0, 3 * C_WIDTH, 3 * C_WIDTH + B_WIDTH, 3 * C_WIDTH + B_WIDTH + C_WIDTH
        w_f_pad = jnp.concatenate([wp["f"], jnp.zeros((D_MODEL, LANES - A_HEADS), F32)], axis=1)
        w_sample = jnp.concatenate([wq.astype(F32), wp["k"], wp["v"], w_f_pad], axis=1).astype(BF16)
        w_gate = wp["gate"].astype(BF16)
        ffn_up = ffn_w_in[l].astype(BF16)
        ffn_dn = ffn_w_out[l].astype(BF16)
        lg = ln_g[l].reshape(3, 1, D_MODEL)
        lb = ln_b[l].reshape(3, 1, D_MODEL)
        fb = fox_f_bias[l]
        a_log_row = jnp.zeros((1, LANES), F32).at[0, :C_HEADS].set(gdn_a_log[l])
        dt_row = jnp.zeros((1, LANES), F32).at[0, :C_HEADS].set(gdn_dt_bias[l])
        norm_w = gdn_norm_w[l].reshape(1, C_HEAD_DIM)
        pw = pool_w[l].astype(BF16)
        psc = pool_scale[l].reshape(1, B_WIDTH)
        w_merge = (w_gate, w_branch_a[l].astype(BF16), w_branch_b[l].astype(BF16),
                   w_branch_c[l].astype(BF16), w_out[l].astype(BF16))

        hp = ffn_ln(yp, ffn_up[0], ffn_dn[0], lg[0], lb[0], tm=tm_p)
        hs = ffn_ln(ys, ffn_up[0], ffn_dn[0], lg[0], lb[0], tm=tm_s)

        q_p, kt_p, ktb_p, vt_p, vtb_p, lft_p, rest_p = inproj_prompt(
            hp, wq, wp["k"].T.astype(BF16), wp["v"].T.astype(BF16), wp["f"].T.astype(BF16),
            fb.reshape(A_HEADS, 1), w_rest, batch=bp, seq=seq, tm=256)
        c_p = cumsum_lanes(lft_p, tb=512)
        oa_p = fox_prompt(q_p, ktb_p, vtb_p, c_p, tq=512).reshape(mp, A_WIDTH)
        rest3 = rest_p.reshape(bp, seq, -1)
        ub_p = rest3[:, :, o_ub:o_ub + B_WIDTH]
        qkvc_p = rest3[:, :, o_qkvc:o_qkvc + 3 * C_WIDTH]
        ob_p = pool_mix(ub_p, jnp.zeros((bp, POOL_BUF + 1, B_WIDTH), F32), pw, psc,
                        tt=512, n_valid=0, out_dtype=BF16).reshape(mp, B_WIDTH)
        oc_p, ssm_p = gdn_mixer(qkvc_p, rest3[:, :, o_ab:o_ab + LANES], rest3[:, :, o_z:o_z + C_WIDTH],
                                jnp.zeros((bp, 8, 3 * C_WIDTH), F32),
                                jnp.zeros((bp, C_HEADS, C_HEAD_DIM, C_HEAD_DIM), F32),
                                gdn_conv_w[l], a_log_row, dt_row, norm_w, chunk=64, t_valid=seq, out_dtype=BF16)
        yp = merge_ln(hp, oa_p, ob_p, oc_p.reshape(mp, C_WIDTH), *w_merge, lg[1], lb[1], tm=tm_p)
        yp = ffn_ln(yp, ffn_up[1], ffn_dn[1], lg[2], lb[2], tm=tm_p)
        k_p = kt_p.reshape(bp, A_HEADS, A_HEAD_DIM, seq).transpose(0, 3, 1, 2)
        v_p = vt_p.reshape(bp, A_HEADS, A_HEAD_DIM, seq).transpose(0, 3, 1, 2)
        for lst, a in zip(outs_p, (k_p, v_p, lft_p.transpose(0, 2, 1), ub_p[:, seq - POOL_BUF:],
                                   qkvc_p[:, seq - CONV_BUF:], ssm_p)):
            lst.append(a)

        proj_s = matmul_rows(hs, w_sample, tm=tm_s)
        rest_s = matmul_rows(hs, w_rest, tm=tm_s).reshape(bd, n_q, -1)
        q_s = proj_s[:, :A_WIDTH].reshape(bd, n_q, A_WIDTH)
        k_s = proj_s[:, A_WIDTH:2 * A_WIDTH].reshape(bd, n_q, A_WIDTH)
        v_s = proj_s[:, 2 * A_WIDTH:3 * A_WIDTH].reshape(bd, n_q, A_WIDTH)
        lf_s = jax.nn.log_sigmoid(proj_s[:, 3 * A_WIDTH:3 * A_WIDTH + A_HEADS] + fb).reshape(bd, n_q, A_HEADS)
        oa_s = fox_sample(page_table, q_s, k_s, v_s, lf_s.transpose(0, 2, 1), cache_kt, cache_vt, cache_lft,
                          layer=l).reshape(ms, A_WIDTH)
        ub_s = rest_s[:, :, o_ub:o_ub + B_WIDTH]
        qkvc_s = rest_s[:, :, o_qkvc:o_qkvc + 3 * C_WIDTH]
        pool_full = jnp.concatenate([state_pool[l], ub_s], axis=1)
        conv_full = jnp.concatenate([state_conv[l], qkvc_s], axis=1)
        pre16 = jnp.concatenate([jnp.zeros((bd, 1, B_WIDTH), F32), state_pool[l]], axis=1)
        ob_s = pool_mix(ub_s, pre16, pw, psc, tt=n_q, n_valid=POOL_BUF, out_dtype=F32).reshape(ms, B_WIDTH)
        pad_t = lambda a: jnp.pad(a, ((0, 0), (0, 8 - n_q), (0, 0)))
        pre8 = jnp.concatenate([jnp.zeros((bd, 8 - CONV_BUF, 3 * C_WIDTH), F32), state_conv[l]], axis=1)
        oc_s, ssm_s = gdn_mixer(pad_t(qkvc_s), pad_t(rest_s[:, :, o_ab:o_ab + LANES]),
                                pad_t(rest_s[:, :, o_z:o_z + C_WIDTH]), pre8, state_ssm[l],
                                gdn_conv_w[l], a_log_row, dt_row, norm_w, chunk=8, t_valid=n_q, out_dtype=F32)
        ys = merge_ln(hs, oa_s, ob_s, oc_s[:, :n_q].reshape(ms, C_WIDTH), *w_merge, lg[1], lb[1], tm=tm_s)
        ys = ffn_ln(ys, ffn_up[1], ffn_dn[1], lg[2], lb[2], tm=tm_s)
        for lst, a in zip(outs_s, (k_s.reshape(bd, n_q, A_HEADS, A_HEAD_DIM), v_s.reshape(bd, n_q, A_HEADS, A_HEAD_DIM),
                                   lf_s, pool_full[:, -POOL_BUF:], conv_full[:, -CONV_BUF:], ssm_s)):
            lst.append(a)

    k_p, v_p, lf_p, pool_p, conv_p, ssm_p = [jnp.stack(a) for a in outs_p]
    k_s, v_s, lf_s, pool_s, conv_s, ssm_s = [jnp.stack(a) for a in outs_s]
    return (yp.reshape(bp, seq, D_MODEL), ys.reshape(bd, n_q, D_MODEL), k_p, v_p, lf_p, pool_p, conv_p, ssm_p,
            k_s, v_s, lf_s, pool_s, conv_s, ssm_s)
```

```python
import functools

import jax
import jax.numpy as jnp
from jax import lax
from jax.experimental import pallas as pl
from jax.experimental.pallas import tpu as pltpu

F32 = jnp.float32
BF16 = jnp.bfloat16
HIGHEST = lax.Precision.HIGHEST

D_MODEL = 1024
DEPTH = 2
PAGE_SIZE = 128
A_HEADS = 16
A_HEAD_DIM = 64
A_WIDTH = A_HEADS * A_HEAD_DIM
POOL_WINDOWS = (2, 4, 8, 16)
POOL_GROUP_DIM = 128
B_WIDTH = len(POOL_WINDOWS) * POOL_GROUP_DIM
POOL_BUF = max(POOL_WINDOWS) - 1
C_HEADS = 4
C_HEAD_DIM = 128
C_WIDTH = C_HEADS * C_HEAD_DIM
CONV_WIDTH = 4
CONV_BUF = CONV_WIDTH - 1
N_BRANCH = 3
D_FF = ((8 * D_MODEL // 3 + 127) // 128) * 128
DN_ALPHA = (2 * DEPTH) ** 0.25
LN_EPS = 1e-5
RMS_EPS = 1e-6
L2_EPS = 1e-6
NEG_BIG = -1e30
LOG2E = 1.4426950408889634

VMEM_LIMIT_BYTES = 56 * 1024 * 1024
LANES = 128


def _params(*sem):
    return pltpu.CompilerParams(dimension_semantics=sem, vmem_limit_bytes=VMEM_LIMIT_BYTES)


def _layer_norm(y, g, b):
    mu = jnp.mean(y, axis=-1, keepdims=True)
    d = y - mu
    var = jnp.mean(d * d, axis=-1, keepdims=True)
    return d * lax.rsqrt(var + LN_EPS) * g + b


def _log_sigmoid(x):
    return jnp.minimum(x, 0.0) - jnp.log(1.0 + jnp.exp(-jnp.abs(x)))


def _softplus(x):
    return jnp.maximum(x, 0.0) + jnp.log(1.0 + jnp.exp(-jnp.abs(x)))


def _sigmoid(x):
    return 1.0 / (1.0 + jnp.exp(-x))


def _silu(x):
    return x * _sigmoid(x)


def _dot(a, b):
    return jnp.dot(a, b, preferred_element_type=F32)


def _dot_nt(a, b, precision=None):
    return lax.dot_general(a, b, (((1,), (1,)), ((), ())), preferred_element_type=F32, precision=precision)


def _dot_hi(a, b):
    return jnp.dot(a, b, preferred_element_type=F32, precision=HIGHEST)


def _split(a):
    hi = a.astype(BF16)
    return hi, (a - hi.astype(F32)).astype(BF16)


def _mm_split(a, b):
    (ah, al), (bh, bl) = a, b
    return _dot(ah, bh) + (_dot(ah, bl) + _dot(al, bh))


def _ffn_ln_kernel(x_ref, wg_ref, wu_ref, wd_ref, g_ref, b_ref, o_ref, acc_ref):
    j = pl.program_id(1)

    @pl.when(j == 0)
    def _():
        acc_ref[...] = jnp.zeros_like(acc_ref)

    xb = x_ref[...].astype(BF16)
    gate = _dot(xb, wg_ref[...])
    up = _dot(xb, wu_ref[...])
    act = (_silu(gate) * up).astype(BF16)
    acc_ref[...] += _dot(act, wd_ref[...])

    @pl.when(j == pl.num_programs(1) - 1)
    def _():
        y = DN_ALPHA * x_ref[...] + 0.5 * acc_ref[...]
        o_ref[...] = _layer_norm(y, g_ref[...], b_ref[...])


def ffn_ln(x, w_up, w_down, g, b, *, tm, n_ff=2):
    m = x.shape[0]
    tf = D_FF // n_ff
    return pl.pallas_call(
        _ffn_ln_kernel,
        grid=(m // tm, n_ff),
        in_specs=[
            pl.BlockSpec((tm, D_MODEL), lambda i, j: (i, 0)),
            pl.BlockSpec((D_MODEL, tf), lambda i, j: (0, j)),
            pl.BlockSpec((D_MODEL, tf), lambda i, j: (0, j + n_ff)),
            pl.BlockSpec((tf, D_MODEL), lambda i, j: (j, 0)),
            pl.BlockSpec((1, D_MODEL), lambda i, j: (0, 0)),
            pl.BlockSpec((1, D_MODEL), lambda i, j: (0, 0)),
        ],
        out_specs=pl.BlockSpec((tm, D_MODEL), lambda i, j: (i, 0)),
        out_shape=jax.ShapeDtypeStruct((m, D_MODEL), F32),
        scratch_shapes=[pltpu.VMEM((tm, D_MODEL), F32)],
        compiler_params=_params("parallel", "arbitrary"),
        name="ffn_ln",
    )(x, w_up, w_up, w_down, g, b)


def _inproj_prompt_kernel(h_ref, wqt_ref, wk_ref, wkt_ref, wvt_ref, wft_ref, fb_ref, wc_ref, wu_ref, wz_ref, wab_ref,
                          qt_ref, kb_ref, kt_ref, vt_ref, vtb_ref, lft_ref, c_ref, u_ref, z_ref, ab_ref):
    hb = h_ref[...].astype(BF16)
    qt_ref[0] = _dot_nt(wqt_ref[...], hb).astype(BF16)
    kb_ref[0] = _dot(hb, wk_ref[...]).astype(BF16)
    kt_ref[0] = _dot_nt(wkt_ref[...], hb)
    vt = _dot_nt(wvt_ref[...], hb)
    vt_ref[0] = vt
    vtb_ref[0] = vt.astype(BF16)
    lft_ref[0] = _log_sigmoid(_dot_nt(wft_ref[...], hb) + fb_ref[...])
    c_ref[0] = _dot(hb, wc_ref[...])
    u_ref[0] = _dot(hb, wu_ref[...])
    z_ref[0] = _dot(hb, wz_ref[...])
    ab_ref[0] = _dot(hb, wab_ref[...])


def inproj_prompt(h, wqt, wk, wkt, wvt, wft, fbias, wc, wu, wz, wab, *, batch, seq, tm):
    m = h.shape[0]
    nt = seq // tm
    const = lambda i: (0, 0)
    tok3 = lambda i: (i // nt, i % nt, 0)
    feat3 = lambda i: (i // nt, 0, i % nt)
    weights = (wqt, wk, wkt, wvt, wft, fbias, wc, wu, wz, wab)
    tok_outs = ((A_WIDTH, BF16), (3 * C_WIDTH, F32), (B_WIDTH, F32), (C_WIDTH, F32), (LANES, F32))
    feat_outs = ((A_WIDTH, BF16), (A_WIDTH, F32), (A_WIDTH, F32), (A_WIDTH, BF16), (A_HEADS, F32))
    specs = ([(feat_outs[0], True), (tok_outs[0], False)] + [(f, True) for f in feat_outs[1:]]
             + [(t, False) for t in tok_outs[1:]])
    return pl.pallas_call(
        _inproj_prompt_kernel,
        grid=(m // tm,),
        in_specs=[pl.BlockSpec((tm, D_MODEL), lambda i: (i, 0))] + [pl.BlockSpec(w.shape, const) for w in weights],
        out_specs=[pl.BlockSpec((1, n, tm), feat3) if feat else pl.BlockSpec((1, tm, n), tok3)
                   for (n, _), feat in specs],
        out_shape=[jax.ShapeDtypeStruct((batch, n, seq) if feat else (batch, seq, n), dt)
                   for (n, dt), feat in specs],
        compiler_params=_params("parallel"),
        name="inproj_prompt",
    )(h, *weights)


def _matmul_kernel(x_ref, w_ref, o_ref):
    o_ref[...] = _dot(x_ref[...].astype(BF16), w_ref[...])


def matmul_rows(x, w, *, tm):
    m, k = x.shape
    n = w.shape[1]
    return pl.pallas_call(
        _matmul_kernel,
        grid=(m // tm,),
        in_specs=[pl.BlockSpec((tm, k), lambda i: (i, 0)), pl.BlockSpec((k, n), lambda i: (0, 0))],
        out_specs=pl.BlockSpec((tm, n), lambda i: (i, 0)),
        out_shape=jax.ShapeDtypeStruct((m, n), F32),
        compiler_params=_params("parallel"),
        name="matmul_rows",
    )(x, w)


def _cumsum_lanes_kernel(x_ref, o_ref, carry_ref, *, tb):
    @pl.when(pl.program_id(1) == 0)
    def _():
        carry_ref[...] = jnp.zeros_like(carry_ref)

    h = x_ref.shape[1]
    r = lax.broadcasted_iota(jnp.int32, (tb, tb), 0)
    c = lax.broadcasted_iota(jnp.int32, (tb, tb), 1)
    upper = (r <= c).astype(F32)
    out = _dot_hi(x_ref[0], upper) + carry_ref[...]
    carry_ref[...] = out[:, tb - 1:tb]
    cols = jnp.concatenate([out * LOG2E, jnp.zeros((LANES - h, tb), F32)], axis=0).T
    for i in range(h):
        o_ref[0, i] = cols[:, i:i + 1]


def cumsum_lanes(x, *, tb):
    b, h, t = x.shape
    return pl.pallas_call(
        functools.partial(_cumsum_lanes_kernel, tb=tb),
        grid=(b, t // tb),
        in_specs=[pl.BlockSpec((1, h, tb), lambda i, j: (i, 0, j))],
        out_specs=pl.BlockSpec((1, h, tb, 1), lambda i, j: (i, 0, j, 0)),
        out_shape=jax.ShapeDtypeStruct((b, h, t, 1), F32),
        scratch_shapes=[pltpu.VMEM((h, 1), F32)],
        compiler_params=_params("parallel", "arbitrary"),
        name="cumsum_lanes",
    )(x)


def _fox_prompt_kernel(qt_ref, k_ref, vt_ref, c_ref, o_ref, km_ref, *, tq):
    i = pl.program_id(2)
    half = lax.broadcasted_iota(jnp.int32, (LANES, tq), 0) < A_HEAD_DIM

    @pl.when(i == 0)
    def _():
        kb = k_ref[0]
        low = lax.broadcasted_iota(jnp.int32, kb.shape, 1) < A_HEAD_DIM
        zero = jnp.zeros_like(kb)
        km_ref[0] = jnp.where(low, kb, zero)
        km_ref[1] = jnp.where(low, zero, kb)

    qt = qt_ref[0]
    key = lax.broadcasted_iota(jnp.int32, (tq, tq), 0)
    qry = lax.broadcasted_iota(jnp.int32, (tq, tq), 1)
    causal = key <= qry

    def step(j, carry, masked):
        off = pl.multiple_of(j * tq, tq)
        vt = vt_ref[0, :, pl.ds(off, tq)]
        scores = [_dot(km_ref[hh, pl.ds(off, tq), :], qt) for hh in range(2)]
        probs = []
        for hh in range(2):
            m, l, _ = carry[hh]
            s = scores[hh] - c_ref[0, hh, pl.ds(off, tq), :]
            if masked:
                s = jnp.where(causal, s, NEG_BIG)
            m_new = jnp.maximum(m, jnp.max(s, axis=0, keepdims=True))
            alpha = jnp.exp2(m - m_new)
            p = jnp.exp2(s - m_new)
            probs.append((m_new, alpha * l + jnp.sum(p, axis=0, keepdims=True), alpha, p.astype(BF16)))
        return tuple((m_new, l, alpha * carry[hh][2] + _dot(vt, p))
                     for hh, (m_new, l, alpha, p) in enumerate(probs))

    one = (jnp.full((1, tq), NEG_BIG, F32), jnp.zeros((1, tq), F32), jnp.zeros((LANES, tq), F32))
    carry = lax.fori_loop(0, i, lambda j, c: step(j, c, False), (one, one))
    (_, l0, a0), (_, l1, a1) = step(i, carry, True)
    o_ref[0] = jnp.where(half, a0 / l0, a1 / l1).T.astype(o_ref.dtype)


def fox_prompt(qt, k, vt, c, *, tq):
    b, t, _ = k.shape
    pairs = A_HEADS // 2
    return pl.pallas_call(
        functools.partial(_fox_prompt_kernel, tq=tq),
        grid=(b, pairs, t // tq),
        in_specs=[
            pl.BlockSpec((1, LANES, tq), lambda bi, hp, i: (bi, hp, i)),
            pl.BlockSpec((1, t, LANES), lambda bi, hp, i: (bi, 0, hp)),
            pl.BlockSpec((1, LANES, t), lambda bi, hp, i: (bi, hp, 0)),
            pl.BlockSpec((1, 2, t, 1), lambda bi, hp, i: (bi, hp, 0, 0)),
        ],
        out_specs=pl.BlockSpec((1, tq, LANES), lambda bi, hp, i: (bi, i, hp)),
        out_shape=jax.ShapeDtypeStruct((b, t, A_WIDTH), BF16),
        scratch_shapes=[pltpu.VMEM((2, t, LANES), BF16)],
        compiler_params=_params("parallel", "parallel", "arbitrary"),
        name="fox_prompt",
    )(qt, k, vt, c)


def _fox_sample_kernel(pt_ref, q_ref, kn_ref, vn_ref, lfn_ref, *rest, n_q, group):
    kp_refs, vp_refs, lfp_refs = rest[:group], rest[group:2 * group], rest[2 * group:3 * group]
    o_ref, qbd_ref, m_ref, l_ref, acc_ref, carry_ref = rest[3 * group:]
    p = pl.program_id(1)
    rows = n_q * A_HEADS
    hrow = lax.broadcasted_iota(jnp.int32, (A_HEADS, A_WIDTH), 0)
    hcol = lax.broadcasted_iota(jnp.int32, (A_HEADS, A_WIDTH), 1) // A_HEAD_DIM
    head_mask = hrow == hcol

    @pl.when(p == 0)
    def _():
        m_ref[...] = jnp.full_like(m_ref, NEG_BIG)
        l_ref[...] = jnp.zeros_like(l_ref)
        acc_ref[...] = jnp.zeros_like(acc_ref)
        carry_ref[...] = jnp.zeros_like(carry_ref)
        for qi in range(n_q):
            qrow = jnp.broadcast_to(q_ref[0, qi:qi + 1, :], (A_HEADS, A_WIDTH))
            qbd_ref[qi * A_HEADS:(qi + 1) * A_HEADS, :] = jnp.where(head_mask, qrow, 0.0)

    qbd = qbd_ref[...].astype(BF16)
    r = lax.broadcasted_iota(jnp.int32, (PAGE_SIZE, PAGE_SIZE), 0)
    c = lax.broadcasted_iota(jnp.int32, (PAGE_SIZE, PAGE_SIZE), 1)
    after = (r > c).astype(F32)
    carry = carry_ref[...]
    scores = []
    for g in range(group):
        lf = lfp_refs[g][...]
        bias = _dot_hi(lf, after) + carry
        carry = carry + jnp.sum(lf, axis=-1, keepdims=True)
        s = _dot(qbd, kp_refs[g][...].astype(BF16))
        scores.append(s + jnp.concatenate([bias] * n_q, axis=0))
    carry_ref[...] = carry
    m_old = m_ref[...]
    m_new = m_old
    for s in scores:
        m_new = jnp.maximum(m_new, jnp.max(s, axis=-1, keepdims=True))
    alpha = jnp.exp(m_old - m_new)
    l_new = alpha * l_ref[...]
    pv = None
    for g, s in enumerate(scores):
        pr = jnp.exp(s - m_new)
        l_new = l_new + jnp.sum(pr, axis=-1, keepdims=True)
        t = _dot_nt(pr.astype(BF16), vp_refs[g][...].astype(BF16))
        pv = t if pv is None else pv + t
    l_ref[...] = l_new
    acc_ref[...] = alpha * acc_ref[...] + pv
    m_ref[...] = m_new

    @pl.when(p == pl.num_programs(1) - 1)
    def _():
        qbd = qbd_ref[...]
        lfn = lfn_ref[0]
        qidx = lax.broadcasted_iota(jnp.int32, (rows, 1), 0) // A_HEADS
        s_new = []
        cum = jnp.zeros((A_HEADS, 1), F32)
        for ki in range(n_q):
            cum = cum + lfn[:, ki:ki + 1]
            sk = jnp.sum(qbd * kn_ref[0, ki:ki + 1, :], axis=-1, keepdims=True)
            sk = sk - jnp.concatenate([cum] * n_q, axis=0)
            s_new.append(jnp.where(qidx >= ki, sk, NEG_BIG))
        m_old = m_ref[...]
        m_fin = m_old
        for sk in s_new:
            m_fin = jnp.maximum(m_fin, sk)
        alpha = jnp.exp(m_old - m_fin)
        l_fin = alpha * l_ref[...]
        acc = alpha * acc_ref[...]
        for ki, sk in enumerate(s_new):
            pk = jnp.exp(sk - m_fin)
            l_fin = l_fin + pk
            acc = acc + pk * vn_ref[0, ki:ki + 1, :]
        acc = acc / l_fin
        for qi in range(n_q):
            blk = jnp.where(head_mask, acc[qi * A_HEADS:(qi + 1) * A_HEADS, :], 0.0)
            o_ref[0, qi:qi + 1, :] = jnp.sum(blk, axis=0, keepdims=True)


def fox_sample(page_table, q, k_new, v_new, lf_new_t, cache_kt, cache_vt, cache_lft, *, layer, group):
    bd, n_q, _ = q.shape
    n_pages = page_table.shape[1]
    rows = n_q * A_HEADS
    pt = page_table.reshape(-1)
    assert n_pages % group == 0

    def page(g):
        return lambda b, p, pt_ref: (layer, pt_ref[b * n_pages + (n_pages - 1 - (p * group + g))], 0, 0)

    seq3 = lambda b, p, pt_ref: (b, 0, 0)
    grid_spec = pltpu.PrefetchScalarGridSpec(
        num_scalar_prefetch=1,
        grid=(bd, n_pages // group),
        in_specs=[
            pl.BlockSpec((1, n_q, A_WIDTH), seq3),
            pl.BlockSpec((1, n_q, A_WIDTH), seq3),
            pl.BlockSpec((1, n_q, A_WIDTH), seq3),
            pl.BlockSpec((1, A_HEADS, n_q), seq3),
        ] + [pl.BlockSpec((None, None, A_WIDTH, PAGE_SIZE), page(g)) for g in range(group)] * 2
          + [pl.BlockSpec((None, None, A_HEADS, PAGE_SIZE), page(g)) for g in range(group)],
        out_specs=pl.BlockSpec((1, n_q, A_WIDTH), seq3),
        scratch_shapes=[
            pltpu.VMEM((rows, A_WIDTH), F32),
            pltpu.VMEM((rows, 1), F32),
            pltpu.VMEM((rows, 1), F32),
            pltpu.VMEM((rows, A_WIDTH), F32),
            pltpu.VMEM((A_HEADS, 1), F32),
        ],
    )
    return pl.pallas_call(
        functools.partial(_fox_sample_kernel, n_q=n_q, group=group),
        grid_spec=grid_spec,
        out_shape=jax.ShapeDtypeStruct((bd, n_q, A_WIDTH), F32),
        compiler_params=_params("parallel", "arbitrary"),
        name="fox_sample",
    )(pt, q, k_new, v_new, lf_new_t, *([cache_kt] * group + [cache_vt] * group + [cache_lft] * group))


def _pool_kernel(u_ref, pre_ref, w_ref, sc_ref, o_ref, ext_ref, *, tt, n_valid):
    halo = POOL_BUF + 1
    j = pl.program_id(1)

    @pl.when(j == 0)
    def _():
        ext_ref[0:halo, :] = pre_ref[0]

    @pl.when(j > 0)
    def _():
        ext_ref[0:halo, :] = ext_ref[tt:tt + halo, :]

    u = u_ref[0]
    ext_ref[halo:halo + tt, :] = u
    pos = j * tt + lax.broadcasted_iota(jnp.int32, (tt, 1), 0) + (1 + n_valid)
    outs = []
    for gi, w in enumerate(POOL_WINDOWS):
        sl = slice(gi * POOL_GROUP_DIM, (gi + 1) * POOL_GROUP_DIM)
        tot = u[:, sl]
        for k in range(1, w):
            tot = tot + ext_ref[halo - k:halo - k + tt, sl]
        cnt = jnp.minimum(pos, w).astype(F32)
        d = tot / cnt - u[:, sl]
        outs.append(_dot(d.astype(BF16), w_ref[gi]))
    o_ref[0] = (jnp.concatenate(outs, axis=-1) * sc_ref[...]).astype(o_ref.dtype)


def pool_mix(u, prefix16, pool_w, pool_scale, *, tt, n_valid, out_dtype):
    b, t, _ = u.shape
    halo = POOL_BUF + 1
    return pl.pallas_call(
        functools.partial(_pool_kernel, tt=tt, n_valid=n_valid),
        grid=(b, t // tt),
        in_specs=[
            pl.BlockSpec((1, tt, B_WIDTH), lambda i, j: (i, j, 0)),
            pl.BlockSpec((1, halo, B_WIDTH), lambda i, j: (i, 0, 0)),
            pl.BlockSpec((len(POOL_WINDOWS), POOL_GROUP_DIM, POOL_GROUP_DIM), lambda i, j: (0, 0, 0)),
            pl.BlockSpec((1, B_WIDTH), lambda i, j: (0, 0)),
        ],
        out_specs=pl.BlockSpec((1, tt, B_WIDTH), lambda i, j: (i, j, 0)),
        out_shape=jax.ShapeDtypeStruct((b, t, B_WIDTH), out_dtype),
        scratch_shapes=[pltpu.VMEM((tt + 2 * halo, B_WIDTH), F32)],
        compiler_params=_params("parallel", "arbitrary"),
        name="pool_mix",
    )(u, prefix16, pool_w, pool_scale)


def _gdn_kernel(x_ref, ab_ref, z_ref, pre_ref, s0_ref, cw_ref, alog_ref, dtb_ref, nw_ref,
                o_ref, s_out_ref, ext_ref, s_ref, *, chunk, t_valid):
    j = pl.program_id(1)
    halo = 8

    @pl.when(j == 0)
    def _():
        ext_ref[0:halo, :] = pre_ref[0]
        s_ref[...] = s0_ref[0]

    @pl.when(j > 0)
    def _():
        ext_ref[0:halo, :] = ext_ref[chunk:chunk + halo, :]

    ext_ref[halo:halo + chunk, :] = x_ref[0]
    y = ext_ref[halo:halo + chunk, :] * cw_ref[CONV_WIDTH - 1:CONV_WIDTH, :]
    for k in range(1, CONV_WIDTH):
        y = y + ext_ref[halo - k:halo - k + chunk, :] * cw_ref[CONV_WIDTH - 1 - k:CONV_WIDTH - k, :]
    y = _silu(y)

    ab = ab_ref[0]
    valid = (j * chunk + lax.broadcasted_iota(jnp.int32, (chunk, 1), 0)) < t_valid
    g_all = jnp.where(valid, -jnp.exp(alog_ref[...]) * _softplus(ab + dtb_ref[...]), 0.0)
    beta_all = jnp.where(valid, _sigmoid(ab), 0.0)
    r = lax.broadcasted_iota(jnp.int32, (chunk, chunk), 0)
    c = lax.broadcasted_iota(jnp.int32, (chunk, chunk), 1)
    gcum = _dot_hi((r >= c).astype(F32), g_all)

    n = C_HEADS * chunk
    rows = [slice(h * chunk, (h + 1) * chunk) for h in range(C_HEADS)]
    heads = lambda off: jnp.concatenate(
        [y[:, off + h * C_HEAD_DIM:off + (h + 1) * C_HEAD_DIM] for h in range(C_HEADS)], axis=0)
    qs, ks, vs = heads(0), heads(C_WIDTH), heads(2 * C_WIDTH)
    qs = qs * lax.rsqrt(jnp.sum(qs * qs, axis=-1, keepdims=True) + L2_EPS) * (C_HEAD_DIM ** -0.5)
    ks = ks * lax.rsqrt(jnp.sum(ks * ks, axis=-1, keepdims=True) + L2_EPS)
    row_head = lax.broadcasted_iota(jnp.int32, (n, LANES), 0) // chunk
    lane = lax.broadcasted_iota(jnp.int32, (n, LANES), 1)
    pick = lambda a, off: jnp.where(lane == row_head + off, jnp.concatenate([a] * C_HEADS, axis=0), 0.0)
    gd = pick(gcum, 0)
    gc = jnp.sum(gd, axis=-1, keepdims=True)
    gr = _dot_nt(jnp.ones((8, LANES), F32), gd, precision=HIGHEST)[0:1, :]
    beta = jnp.sum(pick(beta_all, C_HEADS), axis=-1, keepdims=True)
    g_last = jnp.sum(pick(jnp.broadcast_to(gcum[chunk - 1:chunk, :], (chunk, LANES)), 0), axis=-1, keepdims=True)
    rr = lax.broadcasted_iota(jnp.int32, (n, n), 0)
    cc = lax.broadcasted_iota(jnp.int32, (n, n), 1)
    same = (rr // chunk) == (cc // chunk)
    gamma = jnp.exp(jnp.where(same & (rr >= cc), gc - gr, -jnp.inf))
    eg = jnp.exp(gc)
    ksb = ks.astype(BF16)
    nmat = jnp.where(same & (rr > cc), beta * _dot_nt(ksb, ksb) * gamma, 0.0)
    inv = (rr == cc).astype(F32) - nmat
    pw = _split(nmat)
    for _ in range(max(1, (chunk - 1).bit_length()) - 1):
        pw = _split(_mm_split(pw, pw))
        inv = inv + _mm_split(_split(inv), pw)
    inv = _split(inv)
    u = _mm_split(inv, _split(beta * vs))
    w = _mm_split(inv, _split(beta * ks * eg))
    qk = _dot_nt(qs.astype(BF16), ksb) * gamma
    k_dec = ks * jnp.exp(g_last - gc)
    q_dec = qs * eg
    s_old = [s_ref[h] for h in range(C_HEADS)]
    s_oldb = [s.astype(BF16) for s in s_old]
    v_new = jnp.concatenate(
        [u[rows[h]] - _dot(w[rows[h]].astype(BF16), s_oldb[h]) for h in range(C_HEADS)], axis=0)
    o = jnp.concatenate([_dot(q_dec[rows[h]].astype(BF16), s_oldb[h]) for h in range(C_HEADS)], axis=0)
    o = o + _dot(qk.astype(BF16), v_new.astype(BF16))
    for h in range(C_HEADS):
        s_ref[h] = s_old[h] * jnp.exp(gcum[chunk - 1:chunk, h:h + 1]) + lax.dot_general(
            k_dec[rows[h]].astype(BF16), v_new[rows[h]].astype(BF16), (((0,), (0,)), ((), ())),
            preferred_element_type=F32)
    o = o * lax.rsqrt(jnp.mean(o * o, axis=-1, keepdims=True) + RMS_EPS) * nw_ref[...]
    z = z_ref[0]
    o = o * _silu(jnp.concatenate([z[:, h * C_HEAD_DIM:(h + 1) * C_HEAD_DIM] for h in range(C_HEADS)], axis=0))
    o_ref[0] = jnp.concatenate([o[rows[h]] for h in range(C_HEADS)], axis=-1).astype(o_ref.dtype)

    @pl.when(j == pl.num_programs(1) - 1)
    def _():
        s_out_ref[0] = s_ref[...]


def gdn_mixer(x, ab, z, prefix8, s0, conv_w, a_log_row, dt_bias_row, norm_w, *, chunk, t_valid, out_dtype):
    b, t, _ = x.shape
    row = lambda i, j: (0, 0)
    tok = lambda i, j: (i, j, 0)
    seq = lambda i, j: (i, 0, 0)
    return pl.pallas_call(
        functools.partial(_gdn_kernel, chunk=chunk, t_valid=t_valid),
        grid=(b, t // chunk),
        in_specs=[
            pl.BlockSpec((1, chunk, 3 * C_WIDTH), tok),
            pl.BlockSpec((1, chunk, LANES), tok),
            pl.BlockSpec((1, chunk, C_WIDTH), tok),
            pl.BlockSpec((1, 8, 3 * C_WIDTH), seq),
            pl.BlockSpec((1, C_HEADS, C_HEAD_DIM, C_HEAD_DIM), lambda i, j: (i, 0, 0, 0)),
            pl.BlockSpec((CONV_WIDTH, 3 * C_WIDTH), row),
            pl.BlockSpec((1, LANES), row),
            pl.BlockSpec((1, LANES), row),
            pl.BlockSpec((1, C_HEAD_DIM), row),
        ],
        out_specs=[
            pl.BlockSpec((1, chunk, C_WIDTH), tok),
            pl.BlockSpec((1, C_HEADS, C_HEAD_DIM, C_HEAD_DIM), lambda i, j: (i, 0, 0, 0)),
        ],
        out_shape=[
            jax.ShapeDtypeStruct((b, t, C_WIDTH), out_dtype),
            jax.ShapeDtypeStruct((b, C_HEADS, C_HEAD_DIM, C_HEAD_DIM), F32),
        ],
        scratch_shapes=[
            pltpu.VMEM((chunk + 16, 3 * C_WIDTH), F32),
            pltpu.VMEM((C_HEADS, C_HEAD_DIM, C_HEAD_DIM), F32),
        ],
        compiler_params=_params("parallel", "arbitrary"),
        name="gdn_mixer",
    )(x, ab, z, prefix8, s0, conv_w, a_log_row, dt_bias_row, norm_w)


def _merge_kernel(h_ref, oa_ref, ob_ref, oc_ref, wg_ref, wa_ref, wb_ref, wc_ref, wo_ref, g_ref, b_ref, o_ref):
    h = h_ref[...]
    hb = h.astype(BF16)
    merged = None
    for i, (x_ref, w_ref) in enumerate(((oa_ref, wa_ref), (ob_ref, wb_ref), (oc_ref, wc_ref))):
        gate = _sigmoid(_dot(hb, wg_ref[:, i * D_MODEL:(i + 1) * D_MODEL]))
        term = gate * _dot(x_ref[...].astype(BF16), w_ref[...])
        merged = term if merged is None else merged + term
    y = DN_ALPHA * h + _dot(merged.astype(BF16), wo_ref[...])
    o_ref[...] = _layer_norm(y, g_ref[...], b_ref[...])


def merge_ln(h, oa, ob, oc, wg, wa, wb, wc, wo, g, b, *, tm):
    m = h.shape[0]
    const = lambda i: (0, 0)
    rows = lambda i: (i, 0)
    return pl.pallas_call(
        _merge_kernel,
        grid=(m // tm,),
        in_specs=[
            pl.BlockSpec((tm, D_MODEL), rows),
            pl.BlockSpec((tm, A_WIDTH), rows),
            pl.BlockSpec((tm, B_WIDTH), rows),
            pl.BlockSpec((tm, C_WIDTH), rows),
            pl.BlockSpec((D_MODEL, N_BRANCH * D_MODEL), const),
            pl.BlockSpec((A_WIDTH, D_MODEL), const),
            pl.BlockSpec((B_WIDTH, D_MODEL), const),
            pl.BlockSpec((C_WIDTH, D_MODEL), const),
            pl.BlockSpec((D_MODEL, D_MODEL), const),
            pl.BlockSpec((1, D_MODEL), const),
            pl.BlockSpec((1, D_MODEL), const),
        ],
        out_specs=pl.BlockSpec((tm, D_MODEL), rows),
        out_shape=jax.ShapeDtypeStruct((m, D_MODEL), F32),
        compiler_params=_params("parallel"),
        name="merge_ln",
    )(h, oa, ob, oc, wg, wa, wb, wc, wo, g, b)


def _split_w_in(w_in_l):
    o = 0
    parts = {}
    for name, n in (("q", A_WIDTH), ("k", A_WIDTH), ("v", A_WIDTH), ("f", A_HEADS), ("ub", B_WIDTH),
                    ("qkvc", 3 * C_WIDTH), ("a", C_HEADS), ("b", C_HEADS), ("z", C_WIDTH),
                    ("gate", N_BRANCH * D_MODEL)):
        parts[name] = w_in_l[:, o:o + n]
        o += n
    return parts


def kernel(x_prompt, x_sample, cache_k, cache_v, cache_logf, page_table, state_pool, state_conv, state_ssm,
           w_in, fox_f_bias, gdn_conv_w, gdn_a_log, gdn_dt_bias, gdn_norm_w, pool_w, pool_scale,
           w_branch_a, w_branch_b, w_branch_c, w_out, ffn_w_in, ffn_w_out, ln_g, ln_b):
    bp, seq, _ = x_prompt.shape
    bd, n_q, _ = x_sample.shape
    mp, ms = bp * seq, bd * n_q
    tm_p = 512
    tm_s = ms

    n_pool = cache_k.shape[1]
    cache_kt = cache_k.transpose(0, 1, 3, 4, 2).reshape(DEPTH, n_pool, A_WIDTH, PAGE_SIZE)
    cache_vt = cache_v.transpose(0, 1, 3, 4, 2).reshape(DEPTH, n_pool, A_WIDTH, PAGE_SIZE)
    cache_lft = cache_logf.transpose(0, 1, 3, 2)

    yp = x_prompt.reshape(mp, D_MODEL)
    ys = x_sample.reshape(ms, D_MODEL)
    outs_p = ([], [], [], [], [], [])
    outs_s = ([], [], [], [], [], [])
    zeros_ab = jnp.zeros((D_MODEL, LANES - 2 * C_HEADS), F32)
    for l in range(DEPTH):
        wp = _split_w_in(w_in[l])
        wq = (wp["q"] * (A_HEAD_DIM ** -0.5)).astype(BF16)
        w_ab = jnp.concatenate([wp["a"], wp["b"], zeros_ab], axis=1)
        w_rest = jnp.concatenate([wp["qkvc"], wp["ub"], wp["z"], w_ab], axis=1).astype(BF16)
        o_qkvc, o_ub, o_z, o_ab = 0, 3 * C_WIDTH, 3 * C_WIDTH + B_WIDTH, 3 * C_WIDTH + B_WIDTH + C_WIDTH
        w_f_pad = jnp.concatenate([wp["f"], jnp.zeros((D_MODEL, LANES - A_HEADS), F32)], axis=1)
        w_sample = jnp.concatenate([wq.astype(F32), wp["k"], wp["v"], w_f_pad], axis=1).astype(BF16)
        w_gate = wp["gate"].astype(BF16)
        ffn_up = ffn_w_in[l].astype(BF16)
        ffn_dn = ffn_w_out[l].astype(BF16)
        lg = ln_g[l].reshape(3, 1, D_MODEL)
        lb = ln_b[l].reshape(3, 1, D_MODEL)
        fb = fox_f_bias[l]
        a_log_row = jnp.zeros((1, LANES), F32).at[0, :C_HEADS].set(gdn_a_log[l])
        dt_row = jnp.zeros((1, LANES), F32).at[0, :C_HEADS].set(gdn_dt_bias[l])
        norm_w = gdn_norm_w[l].reshape(1, C_HEAD_DIM)
        pw = pool_w[l].astype(BF16)
        psc = pool_scale[l].reshape(1, B_WIDTH)
        w_merge = (w_gate, w_branch_a[l].astype(BF16), w_branch_b[l].astype(BF16),
                   w_branch_c[l].astype(BF16), w_out[l].astype(BF16))

        hp = ffn_ln(yp, ffn_up[0], ffn_dn[0], lg[0], lb[0], tm=tm_p)
        hs = ffn_ln(ys, ffn_up[0], ffn_dn[0], lg[0], lb[0], tm=tm_s)

        wqt = (wp["q"] * (A_HEAD_DIM ** -0.5 * LOG2E)).T.astype(BF16)
        qt_p, kb_p, kt_p, vt_p, vtb_p, lft_p, qkvc_p, ub_p, z_p, ab_p = inproj_prompt(
            hp, wqt, wp["k"].astype(BF16), wp["k"].T.astype(BF16), wp["v"].T.astype(BF16), wp["f"].T.astype(BF16),
            fb.reshape(A_HEADS, 1), wp["qkvc"].astype(BF16), wp["ub"].astype(BF16), wp["z"].astype(BF16),
            w_ab.astype(BF16), batch=bp, seq=seq, tm=256)
        c_p = cumsum_lanes(lft_p, tb=512)
        oa_p = fox_prompt(qt_p, kb_p, vtb_p, c_p, tq=512).reshape(mp, A_WIDTH)
        ob_p = pool_mix(ub_p, jnp.zeros((bp, POOL_BUF + 1, B_WIDTH), F32), pw, psc,
                        tt=512, n_valid=0, out_dtype=BF16).reshape(mp, B_WIDTH)
        oc_p, ssm_p = gdn_mixer(qkvc_p, ab_p, z_p,
                                jnp.zeros((bp, 8, 3 * C_WIDTH), F32),
                                jnp.zeros((bp, C_HEADS, C_HEAD_DIM, C_HEAD_DIM), F32),
                                gdn_conv_w[l], a_log_row, dt_row, norm_w, chunk=64, t_valid=seq, out_dtype=BF16)
        yp = merge_ln(hp, oa_p, ob_p, oc_p.reshape(mp, C_WIDTH), *w_merge, lg[1], lb[1], tm=tm_p)
        yp = ffn_ln(yp, ffn_up[1], ffn_dn[1], lg[2], lb[2], tm=tm_p)
        k_p = kt_p.reshape(bp, A_HEADS, A_HEAD_DIM, seq).transpose(0, 3, 1, 2)
        v_p = vt_p.reshape(bp, A_HEADS, A_HEAD_DIM, seq).transpose(0, 3, 1, 2)
        for lst, a in zip(outs_p, (k_p, v_p, lft_p.transpose(0, 2, 1), ub_p[:, seq - POOL_BUF:],
                                   qkvc_p[:, seq - CONV_BUF:], ssm_p)):
            lst.append(a)

        proj_s = matmul_rows(hs, w_sample, tm=tm_s)
        rest_s = matmul_rows(hs, w_rest, tm=tm_s).reshape(bd, n_q, -1)
        q_s = proj_s[:, :A_WIDTH].reshape(bd, n_q, A_WIDTH)
        k_s = proj_s[:, A_WIDTH:2 * A_WIDTH].reshape(bd, n_q, A_WIDTH)
        v_s = proj_s[:, 2 * A_WIDTH:3 * A_WIDTH].reshape(bd, n_q, A_WIDTH)
        lf_s = jax.nn.log_sigmoid(proj_s[:, 3 * A_WIDTH:3 * A_WIDTH + A_HEADS] + fb).reshape(bd, n_q, A_HEADS)
        oa_s = fox_sample(page_table, q_s, k_s, v_s, lf_s.transpose(0, 2, 1), cache_kt, cache_vt, cache_lft,
                          layer=l, group=4).reshape(ms, A_WIDTH)
        ub_s = rest_s[:, :, o_ub:o_ub + B_WIDTH]
        qkvc_s = rest_s[:, :, o_qkvc:o_qkvc + 3 * C_WIDTH]
        pool_full = jnp.concatenate([state_pool[l], ub_s], axis=1)
        conv_full = jnp.concatenate([state_conv[l], qkvc_s], axis=1)
        pre16 = jnp.concatenate([jnp.zeros((bd, 1, B_WIDTH), F32), state_pool[l]], axis=1)
        ob_s = pool_mix(ub_s, pre16, pw, psc, tt=n_q, n_valid=POOL_BUF, out_dtype=F32).reshape(ms, B_WIDTH)
        pad_t = lambda a: jnp.pad(a, ((0, 0), (0, 8 - n_q), (0, 0)))
        pre8 = jnp.concatenate([jnp.zeros((bd, 8 - CONV_BUF, 3 * C_WIDTH), F32), state_conv[l]], axis=1)
        oc_s, ssm_s = gdn_mixer(pad_t(qkvc_s), pad_t(rest_s[:, :, o_ab:o_ab + LANES]),
                                pad_t(rest_s[:, :, o_z:o_z + C_WIDTH]), pre8, state_ssm[l],
                                gdn_conv_w[l], a_log_row, dt_row, norm_w, chunk=8, t_valid=n_q, out_dtype=F32)
        ys = merge_ln(hs, oa_s, ob_s, oc_s[:, :n_q].reshape(ms, C_WIDTH), *w_merge, lg[1], lb[1], tm=tm_s)
        ys = ffn_ln(ys, ffn_up[1], ffn_dn[1], lg[2], lb[2], tm=tm_s)
        for lst, a in zip(outs_s, (k_s.reshape(bd, n_q, A_HEADS, A_HEAD_DIM), v_s.reshape(bd, n_q, A_HEADS, A_HEAD_DIM),
                                   lf_s, pool_full[:, -POOL_BUF:], conv_full[:, -CONV_BUF:], ssm_s)):
            lst.append(a)

    k_p, v_p, lf_p, pool_p, conv_p, ssm_p = [jnp.stack(a) for a in outs_p]
    k_s, v_s, lf_s, pool_s, conv_s, ssm_s = [jnp.stack(a) for a in outs_s]
    return (yp.reshape(bp, seq, D_MODEL), ys.reshape(bd, n_q, D_MODEL), k_p, v_p, lf_p, pool_p, conv_p, ssm_p,
            k_s, v_s, lf_s, pool_s, conv_s, ssm_s)
```

```python
import functools

import jax
import jax.numpy as jnp
from jax import lax
from jax.experimental import pallas as pl
from jax.experimental.pallas import tpu as pltpu

F32 = jnp.float32
BF16 = jnp.bfloat16
HIGHEST = lax.Precision.HIGHEST

D_MODEL = 1024
DEPTH = 2
PAGE_SIZE = 128
A_HEADS = 16
A_HEAD_DIM = 64
A_WIDTH = A_HEADS * A_HEAD_DIM
POOL_WINDOWS = (2, 4, 8, 16)
POOL_GROUP_DIM = 128
B_WIDTH = len(POOL_WINDOWS) * POOL_GROUP_DIM
POOL_BUF = max(POOL_WINDOWS) - 1
C_HEADS = 4
C_HEAD_DIM = 128
C_WIDTH = C_HEADS * C_HEAD_DIM
CONV_WIDTH = 4
CONV_BUF = CONV_WIDTH - 1
N_BRANCH = 3
D_FF = ((8 * D_MODEL // 3 + 127) // 128) * 128
DN_ALPHA = (2 * DEPTH) ** 0.25
LN_EPS = 1e-5
RMS_EPS = 1e-6
L2_EPS = 1e-6
NEG_BIG = -1e30
LOG2E = 1.4426950408889634

VMEM_LIMIT_BYTES = 56 * 1024 * 1024
LANES = 128


def _params(*sem):
    return pltpu.CompilerParams(dimension_semantics=sem, vmem_limit_bytes=VMEM_LIMIT_BYTES)


def _layer_norm(y, g, b):
    mu = jnp.mean(y, axis=-1, keepdims=True)
    d = y - mu
    var = jnp.mean(d * d, axis=-1, keepdims=True)
    return d * lax.rsqrt(var + LN_EPS) * g + b


def _log_sigmoid(x):
    return jnp.minimum(x, 0.0) - jnp.log(1.0 + jnp.exp(-jnp.abs(x)))


def _softplus(x):
    return jnp.maximum(x, 0.0) + jnp.log(1.0 + jnp.exp(-jnp.abs(x)))


def _sigmoid(x):
    return 1.0 / (1.0 + jnp.exp(-x))


def _silu(x):
    return x * _sigmoid(x)


def _dot(a, b):
    return jnp.dot(a, b, preferred_element_type=F32)


def _dot_nt(a, b, precision=None):
    return lax.dot_general(a, b, (((1,), (1,)), ((), ())), preferred_element_type=F32, precision=precision)


def _dot_hi(a, b):
    return jnp.dot(a, b, preferred_element_type=F32, precision=HIGHEST)


def _split(a):
    hi = a.astype(BF16)
    return hi, (a - hi.astype(F32)).astype(BF16)


def _mm_split(a, b):
    (ah, al), (bh, bl) = a, b
    return _dot(ah, bh) + (_dot(ah, bl) + _dot(al, bh))


def _ffn_ln_kernel(x_ref, wg_ref, wu_ref, wd_ref, g_ref, b_ref, o_ref, acc_ref):
    j = pl.program_id(1)

    @pl.when(j == 0)
    def _():
        acc_ref[...] = jnp.zeros_like(acc_ref)

    xb = x_ref[...].astype(BF16)
    gate = _dot(xb, wg_ref[...])
    up = _dot(xb, wu_ref[...])
    act = (_silu(gate) * up).astype(BF16)
    acc_ref[...] += _dot(act, wd_ref[...])

    @pl.when(j == pl.num_programs(1) - 1)
    def _():
        y = DN_ALPHA * x_ref[...] + 0.5 * acc_ref[...]
        o_ref[...] = _layer_norm(y, g_ref[...], b_ref[...])


def ffn_ln(x, w_up, w_down, g, b, *, tm, n_ff=2):
    m = x.shape[0]
    tf = D_FF // n_ff
    return pl.pallas_call(
        _ffn_ln_kernel,
        grid=(m // tm, n_ff),
        in_specs=[
            pl.BlockSpec((tm, D_MODEL), lambda i, j: (i, 0)),
            pl.BlockSpec((D_MODEL, tf), lambda i, j: (0, j)),
            pl.BlockSpec((D_MODEL, tf), lambda i, j: (0, j + n_ff)),
            pl.BlockSpec((tf, D_MODEL), lambda i, j: (j, 0)),
            pl.BlockSpec((1, D_MODEL), lambda i, j: (0, 0)),
            pl.BlockSpec((1, D_MODEL), lambda i, j: (0, 0)),
        ],
        out_specs=pl.BlockSpec((tm, D_MODEL), lambda i, j: (i, 0)),
        out_shape=jax.ShapeDtypeStruct((m, D_MODEL), F32),
        scratch_shapes=[pltpu.VMEM((tm, D_MODEL), F32)],
        compiler_params=_params("parallel", "arbitrary"),
        name="ffn_ln",
    )(x, w_up, w_up, w_down, g, b)


def _inproj_prompt_kernel(h_ref, wqt_ref, wk_ref, wkt_ref, wvt_ref, wft_ref, fb_ref, wc_ref, wu_ref, wz_ref, wab_ref,
                          qt_ref, kb_ref, kt_ref, vt_ref, vtb_ref, lft_ref, qn_ref, kn_ref, c_ref, u_ref, z_ref, ab_ref):
    hb = h_ref[...].astype(BF16)
    tm = hb.shape[0]
    head_sq = lambda a: jnp.sum((a * a).reshape(A_HEADS, A_HEAD_DIM, tm), axis=1)
    qt = _dot_nt(wqt_ref[...], hb).astype(BF16)
    qt_ref[0] = qt
    qn_ref[0] = head_sq(qt.astype(F32))
    kb_ref[0] = _dot(hb, wk_ref[...]).astype(BF16)
    kt = _dot_nt(wkt_ref[...], hb)
    kt_ref[0] = kt
    kn_ref[0] = head_sq(kt)
    vt = _dot_nt(wvt_ref[...], hb)
    vt_ref[0] = vt
    vtb_ref[0] = vt.astype(BF16)
    lft_ref[0] = _log_sigmoid(_dot_nt(wft_ref[...], hb) + fb_ref[...])
    c_ref[0] = _dot(hb, wc_ref[...])
    u_ref[0] = _dot(hb, wu_ref[...])
    z_ref[0] = _dot(hb, wz_ref[...])
    ab_ref[0] = _dot(hb, wab_ref[...])


def inproj_prompt(h, wqt, wk, wkt, wvt, wft, fbias, wc, wu, wz, wab, *, batch, seq, tm):
    m = h.shape[0]
    nt = seq // tm
    const = lambda i: (0, 0)
    tok3 = lambda i: (i // nt, i % nt, 0)
    feat3 = lambda i: (i // nt, 0, i % nt)
    weights = (wqt, wk, wkt, wvt, wft, fbias, wc, wu, wz, wab)
    tok_outs = ((A_WIDTH, BF16), (3 * C_WIDTH, F32), (B_WIDTH, F32), (C_WIDTH, F32), (LANES, F32))
    feat_outs = ((A_WIDTH, BF16), (A_WIDTH, F32), (A_WIDTH, F32), (A_WIDTH, BF16), (A_HEADS, F32),
                 (A_HEADS, F32), (A_HEADS, F32))
    specs = ([(feat_outs[0], True), (tok_outs[0], False)] + [(f, True) for f in feat_outs[1:]]
             + [(t, False) for t in tok_outs[1:]])
    return pl.pallas_call(
        _inproj_prompt_kernel,
        grid=(m // tm,),
        in_specs=[pl.BlockSpec((tm, D_MODEL), lambda i: (i, 0))] + [pl.BlockSpec(w.shape, const) for w in weights],
        out_specs=[pl.BlockSpec((1, n, tm), feat3) if feat else pl.BlockSpec((1, tm, n), tok3)
                   for (n, _), feat in specs],
        out_shape=[jax.ShapeDtypeStruct((batch, n, seq) if feat else (batch, seq, n), dt)
                   for (n, dt), feat in specs],
        compiler_params=_params("parallel"),
        name="inproj_prompt",
    )(h, *weights)


def _matmul_kernel(x_ref, w_ref, o_ref):
    o_ref[...] = _dot(x_ref[...].astype(BF16), w_ref[...])


def matmul_rows(x, w, *, tm):
    m, k = x.shape
    n = w.shape[1]
    return pl.pallas_call(
        _matmul_kernel,
        grid=(m // tm,),
        in_specs=[pl.BlockSpec((tm, k), lambda i: (i, 0)), pl.BlockSpec((k, n), lambda i: (0, 0))],
        out_specs=pl.BlockSpec((tm, n), lambda i: (i, 0)),
        out_shape=jax.ShapeDtypeStruct((m, n), F32),
        compiler_params=_params("parallel"),
        name="matmul_rows",
    )(x, w)


def _cumsum_lanes_kernel(x_ref, o_ref, r_ref, carry_ref, *, tb):
    @pl.when(pl.program_id(1) == 0)
    def _():
        carry_ref[...] = jnp.zeros_like(carry_ref)

    h = x_ref.shape[1]
    r = lax.broadcasted_iota(jnp.int32, (tb, tb), 0)
    c = lax.broadcasted_iota(jnp.int32, (tb, tb), 1)
    upper = (r <= c).astype(F32)
    out = _dot_hi(x_ref[0], upper) + carry_ref[...]
    carry_ref[...] = out[:, tb - 1:tb]
    r_ref[0] = out * LOG2E
    cols = jnp.concatenate([out * LOG2E, jnp.zeros((LANES - h, tb), F32)], axis=0).T
    for i in range(h):
        o_ref[0, i] = cols[:, i:i + 1]


def cumsum_lanes(x, *, tb):
    b, h, t = x.shape
    return pl.pallas_call(
        functools.partial(_cumsum_lanes_kernel, tb=tb),
        grid=(b, t // tb),
        in_specs=[pl.BlockSpec((1, h, tb), lambda i, j: (i, 0, j))],
        out_specs=[pl.BlockSpec((1, h, tb, 1), lambda i, j: (i, 0, j, 0)),
                   pl.BlockSpec((1, h, tb), lambda i, j: (i, 0, j))],
        out_shape=[jax.ShapeDtypeStruct((b, h, t, 1), F32), jax.ShapeDtypeStruct((b, h, t), F32)],
        scratch_shapes=[pltpu.VMEM((h, 1), F32)],
        compiler_params=_params("parallel", "arbitrary"),
        name="cumsum_lanes",
    )(x)


def _fox_prompt_kernel(fast_ref, qt_ref, k_ref, vt_ref, c_ref, cr_ref, qn_ref, kn_ref, o_ref, km_ref, *, tq):
    i = pl.program_id(2)
    fast = fast_ref[(pl.program_id(0) * pl.num_programs(1) + pl.program_id(1)) * pl.num_programs(2) + i]

    @pl.when(i == 0)
    def _():
        kb = k_ref[0]
        low = lax.broadcasted_iota(jnp.int32, kb.shape, 1) < A_HEAD_DIM
        zero = jnp.zeros_like(kb)
        km_ref[0] = jnp.where(low, kb, zero)
        km_ref[1] = jnp.where(low, zero, kb)

    qt = qt_ref[0]
    key = lax.broadcasted_iota(jnp.int32, (tq, tq), 0)
    qry = lax.broadcasted_iota(jnp.int32, (tq, tq), 1)
    causal = key <= qry

    def scores_of(j, masked):
        off = pl.multiple_of(j * tq, tq)
        out = []
        for hh in range(2):
            s = _dot(km_ref[hh, pl.ds(off, tq), :], qt) - c_ref[0, hh, pl.ds(off, tq), :]
            out.append(jnp.where(causal, s, NEG_BIG) if masked else s)
        return out, [vt_ref[0, hh * A_HEAD_DIM:(hh + 1) * A_HEAD_DIM, pl.ds(off, tq)] for hh in range(2)]

    def finish(a0, l0, a1, l1):
        o_ref[0] = jnp.concatenate([a0 / l0, a1 / l1], axis=0).T.astype(o_ref.dtype)

    @pl.when(fast != 0)
    def _():
        ref = [qn_ref[0, 0, hh:hh + 1, :] * kn_ref[0, 0, hh:hh + 1, 0:1] - cr_ref[0, 0, hh:hh + 1, :]
               for hh in range(2)]

        def step(j, carry, masked):
            scores, vt = scores_of(j, masked)
            new = []
            for hh in range(2):
                l, acc = carry[hh]
                p = jnp.exp2(scores[hh] - ref[hh])
                new.append((l + jnp.sum(p, axis=0, keepdims=True), acc + _dot(vt[hh], p.astype(BF16))))
            return tuple(new)

        one = (jnp.zeros((1, tq), F32), jnp.zeros((A_HEAD_DIM, tq), F32))
        carry = lax.fori_loop(0, i, lambda j, c: step(j, c, False), (one, one))
        (l0, a0), (l1, a1) = step(i, carry, True)
        finish(a0, l0, a1, l1)

    @pl.when(fast == 0)
    def _():
        def step(j, carry, masked):
            scores, vt = scores_of(j, masked)
            new = []
            for hh in range(2):
                m, l, acc = carry[hh]
                m_new = jnp.maximum(m, jnp.max(scores[hh], axis=0, keepdims=True))
                alpha = jnp.exp2(m - m_new)
                p = jnp.exp2(scores[hh] - m_new)
                new.append((m_new, alpha * l + jnp.sum(p, axis=0, keepdims=True),
                            alpha * acc + _dot(vt[hh], p.astype(BF16))))
            return tuple(new)

        one = (jnp.full((1, tq), NEG_BIG, F32), jnp.zeros((1, tq), F32), jnp.zeros((A_HEAD_DIM, tq), F32))
        carry = lax.fori_loop(0, i, lambda j, c: step(j, c, False), (one, one))
        (_, l0, a0), (_, l1, a1) = step(i, carry, True)
        finish(a0, l0, a1, l1)


FOX_FAST_BOUND = 50.0


def fox_prompt(qt, k, vt, c_col, c_row, qn2, kn2, *, tq):
    b, t, _ = k.shape
    pairs = A_HEADS // 2
    nq = t // tq
    qn = jnp.sqrt(qn2)
    kn = jnp.sqrt(jnp.max(kn2, axis=-1)) * 1.01 + 1e-6
    fast = jnp.max((qn * kn[..., None]).reshape(b, pairs, 2, nq, tq), axis=(2, 4)) <= FOX_FAST_BOUND
    kn_l = jnp.broadcast_to(kn.reshape(b, pairs, 2, 1), (b, pairs, 2, LANES))
    pair_row = lambda bi, hp, i, f: (bi, hp, 0, i)
    grid_spec = pltpu.PrefetchScalarGridSpec(
        num_scalar_prefetch=1,
        grid=(b, pairs, nq),
        in_specs=[
            pl.BlockSpec((1, LANES, tq), lambda bi, hp, i, f: (bi, hp, i)),
            pl.BlockSpec((1, t, LANES), lambda bi, hp, i, f: (bi, 0, hp)),
            pl.BlockSpec((1, LANES, t), lambda bi, hp, i, f: (bi, hp, 0)),
            pl.BlockSpec((1, 2, t, 1), lambda bi, hp, i, f: (bi, hp, 0, 0)),
            pl.BlockSpec((1, 1, 2, tq), pair_row),
            pl.BlockSpec((1, 1, 2, tq), pair_row),
            pl.BlockSpec((1, 1, 2, LANES), lambda bi, hp, i, f: (bi, hp, 0, 0)),
        ],
        out_specs=pl.BlockSpec((1, tq, LANES), lambda bi, hp, i, f: (bi, i, hp)),
        scratch_shapes=[pltpu.VMEM((2, t, LANES), BF16)],
    )
    return pl.pallas_call(
        functools.partial(_fox_prompt_kernel, tq=tq),
        grid_spec=grid_spec,
        out_shape=jax.ShapeDtypeStruct((b, t, A_WIDTH), BF16),
        compiler_params=_params("parallel", "parallel", "arbitrary"),
        name="fox_prompt",
    )(fast.reshape(-1).astype(jnp.int32), qt, k, vt, c_col, c_row.reshape(b, pairs, 2, t),
      qn.reshape(b, pairs, 2, t), kn_l)


def _fox_sample_kernel(pt_ref, q_ref, kn_ref, vn_ref, lfn_ref, *rest, n_q, group):
    kp_refs, vp_refs, lfp_refs = rest[:group], rest[group:2 * group], rest[2 * group:3 * group]
    o_ref, qbd_ref, m_ref, l_ref, acc_ref, carry_ref = rest[3 * group:]
    p = pl.program_id(1)
    rows = n_q * A_HEADS
    hrow = lax.broadcasted_iota(jnp.int32, (A_HEADS, A_WIDTH), 0)
    hcol = lax.broadcasted_iota(jnp.int32, (A_HEADS, A_WIDTH), 1) // A_HEAD_DIM
    head_mask = hrow == hcol

    @pl.when(p == 0)
    def _():
        m_ref[...] = jnp.full_like(m_ref, NEG_BIG)
        l_ref[...] = jnp.zeros_like(l_ref)
        acc_ref[...] = jnp.zeros_like(acc_ref)
        carry_ref[...] = jnp.zeros_like(carry_ref)
        for qi in range(n_q):
            qrow = jnp.broadcast_to(q_ref[0, qi:qi + 1, :], (A_HEADS, A_WIDTH))
            qbd_ref[qi * A_HEADS:(qi + 1) * A_HEADS, :] = jnp.where(head_mask, qrow, 0.0)

    qbd = qbd_ref[...].astype(BF16)
    r = lax.broadcasted_iota(jnp.int32, (PAGE_SIZE, PAGE_SIZE), 0)
    c = lax.broadcasted_iota(jnp.int32, (PAGE_SIZE, PAGE_SIZE), 1)
    after = (r > c).astype(F32)
    carry = carry_ref[...]
    scores = []
    for g in range(group):
        lf = lfp_refs[g][...]
        bias = _dot_hi(lf, after) + carry
        carry = carry + jnp.sum(lf, axis=-1, keepdims=True)
        s = _dot(qbd, kp_refs[g][...].astype(BF16))
        scores.append(s + jnp.concatenate([bias] * n_q, axis=0))
    carry_ref[...] = carry
    m_old = m_ref[...]
    m_new = m_old
    for s in scores:
        m_new = jnp.maximum(m_new, jnp.max(s, axis=-1, keepdims=True))
    alpha = jnp.exp(m_old - m_new)
    l_new = alpha * l_ref[...]
    pv = None
    for g, s in enumerate(scores):
        pr = jnp.exp(s - m_new)
        l_new = l_new + jnp.sum(pr, axis=-1, keepdims=True)
        t = _dot_nt(pr.astype(BF16), vp_refs[g][...].astype(BF16))
        pv = t if pv is None else pv + t
    l_ref[...] = l_new
    acc_ref[...] = alpha * acc_ref[...] + pv
    m_ref[...] = m_new

    @pl.when(p == pl.num_programs(1) - 1)
    def _():
        qbd = qbd_ref[...]
        lfn = lfn_ref[0]
        qidx = lax.broadcasted_iota(jnp.int32, (rows, 1), 0) // A_HEADS
        s_new = []
        cum = jnp.zeros((A_HEADS, 1), F32)
        for ki in range(n_q):
            cum = cum + lfn[:, ki:ki + 1]
            sk = jnp.sum(qbd * kn_ref[0, ki:ki + 1, :], axis=-1, keepdims=True)
            sk = sk - jnp.concatenate([cum] * n_q, axis=0)
            s_new.append(jnp.where(qidx >= ki, sk, NEG_BIG))
        m_old = m_ref[...]
        m_fin = m_old
        for sk in s_new:
            m_fin = jnp.maximum(m_fin, sk)
        alpha = jnp.exp(m_old - m_fin)
        l_fin = alpha * l_ref[...]
        acc = alpha * acc_ref[...]
        for ki, sk in enumerate(s_new):
            pk = jnp.exp(sk - m_fin)
            l_fin = l_fin + pk
            acc = acc + pk * vn_ref[0, ki:ki + 1, :]
        acc = acc / l_fin
        for qi in range(n_q):
            blk = jnp.where(head_mask, acc[qi * A_HEADS:(qi + 1) * A_HEADS, :], 0.0)
            o_ref[0, qi:qi + 1, :] = jnp.sum(blk, axis=0, keepdims=True)


def fox_sample(page_table, q, k_new, v_new, lf_new_t, cache_kt, cache_vt, cache_lft, *, layer, group):
    bd, n_q, _ = q.shape
    n_pages = page_table.shape[1]
    rows = n_q * A_HEADS
    pt = page_table.reshape(-1)
    assert n_pages % group == 0

    def page(g):
        return lambda b, p, pt_ref: (layer, pt_ref[b * n_pages + (n_pages - 1 - (p * group + g))], 0, 0)

    seq3 = lambda b, p, pt_ref: (b, 0, 0)
    grid_spec = pltpu.PrefetchScalarGridSpec(
        num_scalar_prefetch=1,
        grid=(bd, n_pages // group),
        in_specs=[
            pl.BlockSpec((1, n_q, A_WIDTH), seq3),
            pl.BlockSpec((1, n_q, A_WIDTH), seq3),
            pl.BlockSpec((1, n_q, A_WIDTH), seq3),
            pl.BlockSpec((1, A_HEADS, n_q), seq3),
        ] + [pl.BlockSpec((None, None, A_WIDTH, PAGE_SIZE), page(g)) for g in range(group)] * 2
          + [pl.BlockSpec((None, None, A_HEADS, PAGE_SIZE), page(g)) for g in range(group)],
        out_specs=pl.BlockSpec((1, n_q, A_WIDTH), seq3),
        scratch_shapes=[
            pltpu.VMEM((rows, A_WIDTH), F32),
            pltpu.VMEM((rows, 1), F32),
            pltpu.VMEM((rows, 1), F32),
            pltpu.VMEM((rows, A_WIDTH), F32),
            pltpu.VMEM((A_HEADS, 1), F32),
        ],
    )
    return pl.pallas_call(
        functools.partial(_fox_sample_kernel, n_q=n_q, group=group),
        grid_spec=grid_spec,
        out_shape=jax.ShapeDtypeStruct((bd, n_q, A_WIDTH), F32),
        compiler_params=_params("parallel", "arbitrary"),
        name="fox_sample",
    )(pt, q, k_new, v_new, lf_new_t, *([cache_kt] * group + [cache_vt] * group + [cache_lft] * group))


def _pool_kernel(u_ref, pre_ref, w_ref, sc_ref, o_ref, ext_ref, *, tt, n_valid):
    halo = POOL_BUF + 1
    j = pl.program_id(1)

    @pl.when(j == 0)
    def _():
        ext_ref[0:halo, :] = pre_ref[0]

    @pl.when(j > 0)
    def _():
        ext_ref[0:halo, :] = ext_ref[tt:tt + halo, :]

    u = u_ref[0]
    ext_ref[halo:halo + tt, :] = u
    pos = j * tt + lax.broadcasted_iota(jnp.int32, (tt, 1), 0) + (1 + n_valid)
    outs = []
    for gi, w in enumerate(POOL_WINDOWS):
        sl = slice(gi * POOL_GROUP_DIM, (gi + 1) * POOL_GROUP_DIM)
        tot = u[:, sl]
        for k in range(1, w):
            tot = tot + ext_ref[halo - k:halo - k + tt, sl]
        cnt = jnp.minimum(pos, w).astype(F32)
        d = tot / cnt - u[:, sl]
        outs.append(_dot(d.astype(BF16), w_ref[gi]))
    o_ref[0] = (jnp.concatenate(outs, axis=-1) * sc_ref[...]).astype(o_ref.dtype)


def pool_mix(u, prefix16, pool_w, pool_scale, *, tt, n_valid, out_dtype):
    b, t, _ = u.shape
    halo = POOL_BUF + 1
    return pl.pallas_call(
        functools.partial(_pool_kernel, tt=tt, n_valid=n_valid),
        grid=(b, t // tt),
        in_specs=[
            pl.BlockSpec((1, tt, B_WIDTH), lambda i, j: (i, j, 0)),
            pl.BlockSpec((1, halo, B_WIDTH), lambda i, j: (i, 0, 0)),
            pl.BlockSpec((len(POOL_WINDOWS), POOL_GROUP_DIM, POOL_GROUP_DIM), lambda i, j: (0, 0, 0)),
            pl.BlockSpec((1, B_WIDTH), lambda i, j: (0, 0)),
        ],
        out_specs=pl.BlockSpec((1, tt, B_WIDTH), lambda i, j: (i, j, 0)),
        out_shape=jax.ShapeDtypeStruct((b, t, B_WIDTH), out_dtype),
        scratch_shapes=[pltpu.VMEM((tt + 2 * halo, B_WIDTH), F32)],
        compiler_params=_params("parallel", "arbitrary"),
        name="pool_mix",
    )(u, prefix16, pool_w, pool_scale)


def _gdn_kernel(x_ref, ab_ref, z_ref, pre_ref, s0_ref, cw_ref, alog_ref, dtb_ref, nw_ref,
                o_ref, s_out_ref, ext_ref, s_ref, *, chunk, n_chunks, t_valid):
    j = pl.program_id(1)
    halo = 8

    @pl.when(j == 0)
    def _():
        ext_ref[0:halo, :] = pre_ref[0]
        s_ref[...] = s0_ref[0]

    span = chunk * n_chunks

    @pl.when(j > 0)
    def _():
        ext_ref[0:halo, :] = ext_ref[span:span + halo, :]

    ext_ref[halo:halo + span, :] = x_ref[0]
    y_all = ext_ref[halo:halo + span, :] * cw_ref[CONV_WIDTH - 1:CONV_WIDTH, :]
    for k in range(1, CONV_WIDTH):
        y_all = y_all + ext_ref[halo - k:halo - k + span, :] * cw_ref[CONV_WIDTH - 1 - k:CONV_WIDTH - k, :]
    y_all = _silu(y_all)

    ab = ab_ref[0]
    valid = (j * span + lax.broadcasted_iota(jnp.int32, (span, 1), 0)) < t_valid
    g_all = jnp.where(valid, -jnp.exp(alog_ref[...]) * _softplus(ab + dtb_ref[...]), 0.0)
    beta_all = jnp.where(valid, _sigmoid(ab), 0.0)
    z_all = z_ref[0]
    r = lax.broadcasted_iota(jnp.int32, (chunk, chunk), 0)
    c = lax.broadcasted_iota(jnp.int32, (chunk, chunk), 1)
    tri = (r >= c).astype(F32)

    n = C_HEADS * chunk
    rows = [slice(h * chunk, (h + 1) * chunk) for h in range(C_HEADS)]
    row_head = lax.broadcasted_iota(jnp.int32, (n, LANES), 0) // chunk
    lane = lax.broadcasted_iota(jnp.int32, (n, LANES), 1)
    pick = lambda a, off: jnp.where(lane == row_head + off, jnp.concatenate([a] * C_HEADS, axis=0), 0.0)
    rr = lax.broadcasted_iota(jnp.int32, (n, n), 0)
    cc = lax.broadcasted_iota(jnp.int32, (n, n), 1)
    same = (rr // chunk) == (cc // chunk)

    def setup(ci):
        tok = slice(ci * chunk, (ci + 1) * chunk)
        y = y_all[tok]
        gcum = _dot_hi(tri, g_all[tok])
        heads = lambda off: jnp.concatenate(
            [y[:, off + h * C_HEAD_DIM:off + (h + 1) * C_HEAD_DIM] for h in range(C_HEADS)], axis=0)
        qs, ks, vs = heads(0), heads(C_WIDTH), heads(2 * C_WIDTH)
        qs = qs * lax.rsqrt(jnp.sum(qs * qs, axis=-1, keepdims=True) + L2_EPS) * (C_HEAD_DIM ** -0.5)
        ks = ks * lax.rsqrt(jnp.sum(ks * ks, axis=-1, keepdims=True) + L2_EPS)
        gd = pick(gcum, 0)
        gc = jnp.sum(gd, axis=-1, keepdims=True)
        gr = _dot_nt(jnp.ones((8, LANES), F32), gd, precision=HIGHEST)[0:1, :]
        beta = jnp.sum(pick(beta_all[tok], C_HEADS), axis=-1, keepdims=True)
        g_last = jnp.sum(pick(jnp.broadcast_to(gcum[chunk - 1:chunk, :], (chunk, LANES)), 0),
                         axis=-1, keepdims=True)
        gamma = jnp.exp(jnp.where(same & (rr >= cc), gc - gr, -jnp.inf))
        eg = jnp.exp(gc)
        ksb = ks.astype(BF16)
        return dict(
            nmat=jnp.where(same & (rr > cc), beta * _dot_nt(ksb, ksb) * gamma, 0.0),
            bv=_split(beta * vs), bk=_split(beta * ks * eg),
            qk=(_dot_nt(qs.astype(BF16), ksb) * gamma).astype(BF16),
            k_dec=ks * jnp.exp(g_last - gc), q_dec=qs * eg,
            decay=[jnp.exp(gcum[chunk - 1:chunk, h:h + 1]) for h in range(C_HEADS)],
            z=jnp.concatenate([z_all[tok, h * C_HEAD_DIM:(h + 1) * C_HEAD_DIM] for h in range(C_HEADS)], axis=0))

    parts = [setup(ci) for ci in range(n_chunks)]
    inv = [(rr == cc).astype(F32) - p["nmat"] for p in parts]
    pw = [_split(p["nmat"]) for p in parts]
    for _ in range(max(1, (chunk - 1).bit_length()) - 1):
        pw = [_split(_dot(x[0], x[0])) for x in pw]
        inv = [a + _mm_split(_split(a), x) for a, x in zip(inv, pw)]
    inv = [_split(a) for a in inv]
    for p, a in zip(parts, inv):
        p["u"] = _mm_split(a, p["bv"])
        p["w"] = _mm_split(a, p["bk"])
    state = [s_ref[h] for h in range(C_HEADS)]
    for ci, p in enumerate(parts):
        state_b = [s.astype(BF16) for s in state]
        v_new = jnp.concatenate(
            [p["u"][rows[h]] - _dot(p["w"][rows[h]].astype(BF16), state_b[h]) for h in range(C_HEADS)], axis=0)
        o = jnp.concatenate([_dot(p["q_dec"][rows[h]].astype(BF16), state_b[h]) for h in range(C_HEADS)], axis=0)
        o = o + _dot(p["qk"], v_new.astype(BF16))
        state = [state[h] * p["decay"][h] + lax.dot_general(
            p["k_dec"][rows[h]].astype(BF16), v_new[rows[h]].astype(BF16), (((0,), (0,)), ((), ())),
            preferred_element_type=F32) for h in range(C_HEADS)]
        o = o * lax.rsqrt(jnp.mean(o * o, axis=-1, keepdims=True) + RMS_EPS) * nw_ref[...]
        o = o * _silu(p["z"])
        o_ref[0, ci * chunk:(ci + 1) * chunk, :] = jnp.concatenate(
            [o[rows[h]] for h in range(C_HEADS)], axis=-1).astype(o_ref.dtype)
    for h in range(C_HEADS):
        s_ref[h] = state[h]

    @pl.when(j == pl.num_programs(1) - 1)
    def _():
        s_out_ref[0] = s_ref[...]


def gdn_mixer(x, ab, z, prefix8, s0, conv_w, a_log_row, dt_bias_row, norm_w, *, chunk, n_chunks, t_valid,
              out_dtype):
    b, t, _ = x.shape
    span = chunk * n_chunks
    row = lambda i, j: (0, 0)
    tok = lambda i, j: (i, j, 0)
    seq = lambda i, j: (i, 0, 0)
    return pl.pallas_call(
        functools.partial(_gdn_kernel, chunk=chunk, n_chunks=n_chunks, t_valid=t_valid),
        grid=(b, t // span),
        in_specs=[
            pl.BlockSpec((1, span, 3 * C_WIDTH), tok),
            pl.BlockSpec((1, span, LANES), tok),
            pl.BlockSpec((1, span, C_WIDTH), tok),
            pl.BlockSpec((1, 8, 3 * C_WIDTH), seq),
            pl.BlockSpec((1, C_HEADS, C_HEAD_DIM, C_HEAD_DIM), lambda i, j: (i, 0, 0, 0)),
            pl.BlockSpec((CONV_WIDTH, 3 * C_WIDTH), row),
            pl.BlockSpec((1, LANES), row),
            pl.BlockSpec((1, LANES), row),
            pl.BlockSpec((1, C_HEAD_DIM), row),
        ],
        out_specs=[
            pl.BlockSpec((1, span, C_WIDTH), tok),
            pl.BlockSpec((1, C_HEADS, C_HEAD_DIM, C_HEAD_DIM), lambda i, j: (i, 0, 0, 0)),
        ],
        out_shape=[
            jax.ShapeDtypeStruct((b, t, C_WIDTH), out_dtype),
            jax.ShapeDtypeStruct((b, C_HEADS, C_HEAD_DIM, C_HEAD_DIM), F32),
        ],
        scratch_shapes=[
            pltpu.VMEM((span + 16, 3 * C_WIDTH), F32),
            pltpu.VMEM((C_HEADS, C_HEAD_DIM, C_HEAD_DIM), F32),
        ],
        compiler_params=_params("parallel", "arbitrary"),
        name="gdn_mixer",
    )(x, ab, z, prefix8, s0, conv_w, a_log_row, dt_bias_row, norm_w)


def _merge_kernel(h_ref, oa_ref, ob_ref, oc_ref, wg_ref, wa_ref, wb_ref, wc_ref, wo_ref, g_ref, b_ref, o_ref):
    h = h_ref[...]
    hb = h.astype(BF16)
    merged = None
    for i, (x_ref, w_ref) in enumerate(((oa_ref, wa_ref), (ob_ref, wb_ref), (oc_ref, wc_ref))):
        gate = _sigmoid(_dot(hb, wg_ref[:, i * D_MODEL:(i + 1) * D_MODEL]))
        term = gate * _dot(x_ref[...].astype(BF16), w_ref[...])
        merged = term if merged is None else merged + term
    y = DN_ALPHA * h + _dot(merged.astype(BF16), wo_ref[...])
    o_ref[...] = _layer_norm(y, g_ref[...], b_ref[...])


def merge_ln(h, oa, ob, oc, wg, wa, wb, wc, wo, g, b, *, tm):
    m = h.shape[0]
    const = lambda i: (0, 0)
    rows = lambda i: (i, 0)
    return pl.pallas_call(
        _merge_kernel,
        grid=(m // tm,),
        in_specs=[
            pl.BlockSpec((tm, D_MODEL), rows),
            pl.BlockSpec((tm, A_WIDTH), rows),
            pl.BlockSpec((tm, B_WIDTH), rows),
            pl.BlockSpec((tm, C_WIDTH), rows),
            pl.BlockSpec((D_MODEL, N_BRANCH * D_MODEL), const),
            pl.BlockSpec((A_WIDTH, D_MODEL), const),
            pl.BlockSpec((B_WIDTH, D_MODEL), const),
            pl.BlockSpec((C_WIDTH, D_MODEL), const),
            pl.BlockSpec((D_MODEL, D_MODEL), const),
            pl.BlockSpec((1, D_MODEL), const),
            pl.BlockSpec((1, D_MODEL), const),
        ],
        out_specs=pl.BlockSpec((tm, D_MODEL), rows),
        out_shape=jax.ShapeDtypeStruct((m, D_MODEL), F32),
        compiler_params=_params("parallel"),
        name="merge_ln",
    )(h, oa, ob, oc, wg, wa, wb, wc, wo, g, b)


def _split_w_in(w_in_l):
    o = 0
    parts = {}
    for name, n in (("q", A_WIDTH), ("k", A_WIDTH), ("v", A_WIDTH), ("f", A_HEADS), ("ub", B_WIDTH),
                    ("qkvc", 3 * C_WIDTH), ("a", C_HEADS), ("b", C_HEADS), ("z", C_WIDTH),
                    ("gate", N_BRANCH * D_MODEL)):
        parts[name] = w_in_l[:, o:o + n]
        o += n
    return parts


def kernel(x_prompt, x_sample, cache_k, cache_v, cache_logf, page_table, state_pool, state_conv, state_ssm,
           w_in, fox_f_bias, gdn_conv_w, gdn_a_log, gdn_dt_bias, gdn_norm_w, pool_w, pool_scale,
           w_branch_a, w_branch_b, w_branch_c, w_out, ffn_w_in, ffn_w_out, ln_g, ln_b):
    bp, seq, _ = x_prompt.shape
    bd, n_q, _ = x_sample.shape
    mp, ms = bp * seq, bd * n_q
    tm_p = 512
    tm_s = ms

    n_pool = cache_k.shape[1]
    cache_kt = cache_k.transpose(0, 1, 3, 4, 2).reshape(DEPTH, n_pool, A_WIDTH, PAGE_SIZE)
    cache_vt = cache_v.transpose(0, 1, 3, 4, 2).reshape(DEPTH, n_pool, A_WIDTH, PAGE_SIZE)
    cache_lft = cache_logf.transpose(0, 1, 3, 2)

    yp = x_prompt.reshape(mp, D_MODEL)
    ys = x_sample.reshape(ms, D_MODEL)
    outs_p = ([], [], [], [], [], [])
    outs_s = ([], [], [], [], [], [])
    zeros_ab = jnp.zeros((D_MODEL, LANES - 2 * C_HEADS), F32)
    for l in range(DEPTH):
        wp = _split_w_in(w_in[l])
        wq = (wp["q"] * (A_HEAD_DIM ** -0.5)).astype(BF16)
        w_ab = jnp.concatenate([wp["a"], wp["b"], zeros_ab], axis=1)
        w_rest = jnp.concatenate([wp["qkvc"], wp["ub"], wp["z"], w_ab], axis=1).astype(BF16)
        o_qkvc, o_ub, o_z, o_ab = 0, 3 * C_WIDTH, 3 * C_WIDTH + B_WIDTH, 3 * C_WIDTH + B_WIDTH + C_WIDTH
        w_f_pad = jnp.concatenate([wp["f"], jnp.zeros((D_MODEL, LANES - A_HEADS), F32)], axis=1)
        w_sample = jnp.concatenate([wq.astype(F32), wp["k"], wp["v"], w_f_pad], axis=1).astype(BF16)
        w_gate = wp["gate"].astype(BF16)
        ffn_up = ffn_w_in[l].astype(BF16)
        ffn_dn = ffn_w_out[l].astype(BF16)
        lg = ln_g[l].reshape(3, 1, D_MODEL)
        lb = ln_b[l].reshape(3, 1, D_MODEL)
        fb = fox_f_bias[l]
        a_log_row = jnp.zeros((1, LANES), F32).at[0, :C_HEADS].set(gdn_a_log[l])
        dt_row = jnp.zeros((1, LANES), F32).at[0, :C_HEADS].set(gdn_dt_bias[l])
        norm_w = gdn_norm_w[l].reshape(1, C_HEAD_DIM)
        pw = pool_w[l].astype(BF16)
        psc = pool_scale[l].reshape(1, B_WIDTH)
        w_merge = (w_gate, w_branch_a[l].astype(BF16), w_branch_b[l].astype(BF16),
                   w_branch_c[l].astype(BF16), w_out[l].astype(BF16))

        hp = ffn_ln(yp, ffn_up[0], ffn_dn[0], lg[0], lb[0], tm=tm_p)
        hs = ffn_ln(ys, ffn_up[0], ffn_dn[0], lg[0], lb[0], tm=tm_s)

        wqt = (wp["q"] * (A_HEAD_DIM ** -0.5 * LOG2E)).T.astype(BF16)
        qt_p, kb_p, kt_p, vt_p, vtb_p, lft_p, qn2_p, kn2_p, qkvc_p, ub_p, z_p, ab_p = inproj_prompt(
            hp, wqt, wp["k"].astype(BF16), wp["k"].T.astype(BF16), wp["v"].T.astype(BF16), wp["f"].T.astype(BF16),
            fb.reshape(A_HEADS, 1), wp["qkvc"].astype(BF16), wp["ub"].astype(BF16), wp["z"].astype(BF16),
            w_ab.astype(BF16), batch=bp, seq=seq, tm=256)
        ccol_p, crow_p = cumsum_lanes(lft_p, tb=512)
        oa_p = fox_prompt(qt_p, kb_p, vtb_p, ccol_p, crow_p, qn2_p, kn2_p, tq=512).reshape(mp, A_WIDTH)
        ob_p = pool_mix(ub_p, jnp.zeros((bp, POOL_BUF + 1, B_WIDTH), F32), pw, psc,
                        tt=512, n_valid=0, out_dtype=BF16).reshape(mp, B_WIDTH)
        oc_p, ssm_p = gdn_mixer(qkvc_p, ab_p, z_p,
                                jnp.zeros((bp, 8, 3 * C_WIDTH), F32),
                                jnp.zeros((bp, C_HEADS, C_HEAD_DIM, C_HEAD_DIM), F32),
                                gdn_conv_w[l], a_log_row, dt_row, norm_w, chunk=64, n_chunks=4, t_valid=seq,
                                out_dtype=BF16)
        yp = merge_ln(hp, oa_p, ob_p, oc_p.reshape(mp, C_WIDTH), *w_merge, lg[1], lb[1], tm=tm_p)
        yp = ffn_ln(yp, ffn_up[1], ffn_dn[1], lg[2], lb[2], tm=tm_p)
        k_p = kt_p.reshape(bp, A_HEADS, A_HEAD_DIM, seq).transpose(0, 3, 1, 2)
        v_p = vt_p.reshape(bp, A_HEADS, A_HEAD_DIM, seq).transpose(0, 3, 1, 2)
        for lst, a in zip(outs_p, (k_p, v_p, lft_p.transpose(0, 2, 1), ub_p[:, seq - POOL_BUF:],
                                   qkvc_p[:, seq - CONV_BUF:], ssm_p)):
            lst.append(a)

        proj_s = matmul_rows(hs, w_sample, tm=tm_s)
        rest_s = matmul_rows(hs, w_rest, tm=tm_s).reshape(bd, n_q, -1)
        q_s = proj_s[:, :A_WIDTH].reshape(bd, n_q, A_WIDTH)
        k_s = proj_s[:, A_WIDTH:2 * A_WIDTH].reshape(bd, n_q, A_WIDTH)
        v_s = proj_s[:, 2 * A_WIDTH:3 * A_WIDTH].reshape(bd, n_q, A_WIDTH)
        lf_s = jax.nn.log_sigmoid(proj_s[:, 3 * A_WIDTH:3 * A_WIDTH + A_HEADS] + fb).reshape(bd, n_q, A_HEADS)
        oa_s = fox_sample(page_table, q_s, k_s, v_s, lf_s.transpose(0, 2, 1), cache_kt, cache_vt, cache_lft,
                          layer=l, group=8).reshape(ms, A_WIDTH)
        ub_s = rest_s[:, :, o_ub:o_ub + B_WIDTH]
        qkvc_s = rest_s[:, :, o_qkvc:o_qkvc + 3 * C_WIDTH]
        pool_full = jnp.concatenate([state_pool[l], ub_s], axis=1)
        conv_full = jnp.concatenate([state_conv[l], qkvc_s], axis=1)
        pre16 = jnp.concatenate([jnp.zeros((bd, 1, B_WIDTH), F32), state_pool[l]], axis=1)
        ob_s = pool_mix(ub_s, pre16, pw, psc, tt=n_q, n_valid=POOL_BUF, out_dtype=F32).reshape(ms, B_WIDTH)
        pad_t = lambda a: jnp.pad(a, ((0, 0), (0, 8 - n_q), (0, 0)))
        pre8 = jnp.concatenate([jnp.zeros((bd, 8 - CONV_BUF, 3 * C_WIDTH), F32), state_conv[l]], axis=1)
        oc_s, ssm_s = gdn_mixer(pad_t(qkvc_s), pad_t(rest_s[:, :, o_ab:o_ab + LANES]),
                                pad_t(rest_s[:, :, o_z:o_z + C_WIDTH]), pre8, state_ssm[l],
                                gdn_conv_w[l], a_log_row, dt_row, norm_w, chunk=8, n_chunks=1, t_valid=n_q,
                                out_dtype=F32)
        ys = merge_ln(hs, oa_s, ob_s, oc_s[:, :n_q].reshape(ms, C_WIDTH), *w_merge, lg[1], lb[1], tm=tm_s)
        ys = ffn_ln(ys, ffn_up[1], ffn_dn[1], lg[2], lb[2], tm=tm_s)
        for lst, a in zip(outs_s, (k_s.reshape(bd, n_q, A_HEADS, A_HEAD_DIM), v_s.reshape(bd, n_q, A_HEADS, A_HEAD_DIM),
                                   lf_s, pool_full[:, -POOL_BUF:], conv_full[:, -CONV_BUF:], ssm_s)):
            lst.append(a)

    k_p, v_p, lf_p, pool_p, conv_p, ssm_p = [jnp.stack(a) for a in outs_p]
    k_s, v_s, lf_s, pool_s, conv_s, ssm_s = [jnp.stack(a) for a in outs_s]
    return (yp.reshape(bp, seq, D_MODEL), ys.reshape(bd, n_q, D_MODEL), k_p, v_p, lf_p, pool_p, conv_p, ssm_p,
            k_s, v_s, lf_s, pool_s, conv_s, ssm_s)
```

```python
import functools

import jax
import jax.numpy as jnp
from jax import lax
from jax.experimental import pallas as pl
from jax.experimental.pallas import tpu as pltpu

F32 = jnp.float32
BF16 = jnp.bfloat16
HIGHEST = lax.Precision.HIGHEST

D_MODEL = 1024
DEPTH = 2
PAGE_SIZE = 128
A_HEADS = 16
A_HEAD_DIM = 64
A_WIDTH = A_HEADS * A_HEAD_DIM
POOL_WINDOWS = (2, 4, 8, 16)
POOL_GROUP_DIM = 128
B_WIDTH = len(POOL_WINDOWS) * POOL_GROUP_DIM
POOL_BUF = max(POOL_WINDOWS) - 1
C_HEADS = 4
C_HEAD_DIM = 128
C_WIDTH = C_HEADS * C_HEAD_DIM
CONV_WIDTH = 4
CONV_BUF = CONV_WIDTH - 1
N_BRANCH = 3
D_FF = ((8 * D_MODEL // 3 + 127) // 128) * 128
DN_ALPHA = (2 * DEPTH) ** 0.25
LN_EPS = 1e-5
RMS_EPS = 1e-6
L2_EPS = 1e-6
NEG_BIG = -1e30
LOG2E = 1.4426950408889634

VMEM_LIMIT_BYTES = 56 * 1024 * 1024
LANES = 128


def _params(*sem):
    return pltpu.CompilerParams(dimension_semantics=sem, vmem_limit_bytes=VMEM_LIMIT_BYTES)


def _layer_norm(y, g, b):
    mu = jnp.mean(y, axis=-1, keepdims=True)
    d = y - mu
    var = jnp.mean(d * d, axis=-1, keepdims=True)
    return d * lax.rsqrt(var + LN_EPS) * g + b


def _log_sigmoid(x):
    return jnp.minimum(x, 0.0) - jnp.log(1.0 + jnp.exp(-jnp.abs(x)))


def _softplus(x):
    return jnp.maximum(x, 0.0) + jnp.log(1.0 + jnp.exp(-jnp.abs(x)))


def _sigmoid(x):
    return 1.0 / (1.0 + jnp.exp(-x))


def _silu(x):
    return x * _sigmoid(x)


def _dot(a, b):
    return jnp.dot(a, b, preferred_element_type=F32)


def _dot_nt(a, b, precision=None):
    return lax.dot_general(a, b, (((1,), (1,)), ((), ())), preferred_element_type=F32, precision=precision)


def _dot_hi(a, b):
    return jnp.dot(a, b, preferred_element_type=F32, precision=HIGHEST)


def _split(a):
    hi = a.astype(BF16)
    return hi, (a - hi.astype(F32)).astype(BF16)


def _mm_split(a, b):
    (ah, al), (bh, bl) = a, b
    return _dot(ah, bh) + (_dot(ah, bl) + _dot(al, bh))


FFN_COLUMN_SPLITS = (0, 768, 1536, 2304, D_FF)


def _ffn_ln_kernel(x_ref, wup_ref, wd_ref, g_ref, b_ref, o_ref):
    x = x_ref[...]
    xb = x.astype(BF16)
    groups = list(zip(FFN_COLUMN_SPLITS[:-1], FFN_COLUMN_SPLITS[1:]))
    gate_up = lambda lo, hi: (_dot(xb, wup_ref[:, lo:hi]), _dot(xb, wup_ref[:, D_FF + lo:D_FF + hi]))
    pending = gate_up(*groups[0])
    acc = None
    for gi, (lo, hi) in enumerate(groups):
        gate, up = pending
        if gi + 1 < len(groups):
            pending = gate_up(*groups[gi + 1])
        part = _dot((_silu(gate) * up).astype(BF16), wd_ref[lo:hi, :])
        acc = part if acc is None else acc + part
    o_ref[...] = _layer_norm(DN_ALPHA * x + 0.5 * acc, g_ref[...], b_ref[...])


def ffn_ln(x, w_up, w_down, g, b, *, tm):
    m = x.shape[0]
    const = lambda i: (0, 0)
    resident = dict(pipeline_mode=pl.Buffered(1))
    return pl.pallas_call(
        _ffn_ln_kernel,
        grid=(m // tm,),
        in_specs=[
            pl.BlockSpec((tm, D_MODEL), lambda i: (i, 0)),
            pl.BlockSpec((D_MODEL, 2 * D_FF), const, **resident),
            pl.BlockSpec((D_FF, D_MODEL), const, **resident),
            pl.BlockSpec((1, D_MODEL), const),
            pl.BlockSpec((1, D_MODEL), const),
        ],
        out_specs=pl.BlockSpec((tm, D_MODEL), lambda i: (i, 0)),
        out_shape=jax.ShapeDtypeStruct((m, D_MODEL), F32),
        compiler_params=_params("parallel"),
        name="ffn_ln",
    )(x, w_up, w_down, g, b)


def _inproj_prompt_kernel(h_ref, wqt_ref, wk_ref, wkt_ref, wvt_ref, wft_ref, fb_ref, wc_ref, wu_ref, wz_ref, wab_ref,
                          qt_ref, kb_ref, kt_ref, vt_ref, vtb_ref, lft_ref, qn_ref, kn_ref, c_ref, u_ref, z_ref, ab_ref):
    hb = h_ref[...].astype(BF16)
    tm = hb.shape[0]
    head_sq = lambda a: jnp.sum((a * a).reshape(A_HEADS, A_HEAD_DIM, tm), axis=1)
    qt = _dot_nt(wqt_ref[...], hb).astype(BF16)
    qt_ref[0] = qt
    qn_ref[0] = head_sq(qt.astype(F32))
    kb_ref[0] = _dot(hb, wk_ref[...]).astype(BF16)
    kt = _dot_nt(wkt_ref[...], hb)
    kt_ref[0] = kt
    kn_ref[0] = head_sq(kt)
    vt = _dot_nt(wvt_ref[...], hb)
    vt_ref[0] = vt
    vtb_ref[0] = vt.astype(BF16)
    lft_ref[0] = _log_sigmoid(_dot_nt(wft_ref[...], hb) + fb_ref[...])
    c_ref[0] = _dot(hb, wc_ref[...])
    u_ref[0] = _dot(hb, wu_ref[...])
    z_ref[0] = _dot(hb, wz_ref[...])
    ab_ref[0] = _dot(hb, wab_ref[...])


def inproj_prompt(h, wqt, wk, wkt, wvt, wft, fbias, wc, wu, wz, wab, *, batch, seq, tm):
    m = h.shape[0]
    nt = seq // tm
    const = lambda i: (0, 0)
    tok3 = lambda i: (i // nt, i % nt, 0)
    feat3 = lambda i: (i // nt, 0, i % nt)
    weights = (wqt, wk, wkt, wvt, wft, fbias, wc, wu, wz, wab)
    tok_outs = ((A_WIDTH, BF16), (3 * C_WIDTH, F32), (B_WIDTH, F32), (C_WIDTH, F32), (LANES, F32))
    feat_outs = ((A_WIDTH, BF16), (A_WIDTH, F32), (A_WIDTH, F32), (A_WIDTH, BF16), (A_HEADS, F32),
                 (A_HEADS, F32), (A_HEADS, F32))
    specs = ([(feat_outs[0], True), (tok_outs[0], False)] + [(f, True) for f in feat_outs[1:]]
             + [(t, False) for t in tok_outs[1:]])
    return pl.pallas_call(
        _inproj_prompt_kernel,
        grid=(m // tm,),
        in_specs=[pl.BlockSpec((tm, D_MODEL), lambda i: (i, 0))] + [pl.BlockSpec(w.shape, const) for w in weights],
        out_specs=[pl.BlockSpec((1, n, tm), feat3) if feat else pl.BlockSpec((1, tm, n), tok3)
                   for (n, _), feat in specs],
        out_shape=[jax.ShapeDtypeStruct((batch, n, seq) if feat else (batch, seq, n), dt)
                   for (n, dt), feat in specs],
        compiler_params=_params("parallel"),
        name="inproj_prompt",
    )(h, *weights)


def _matmul_kernel(x_ref, w_ref, o_ref):
    o_ref[...] = _dot(x_ref[...].astype(BF16), w_ref[...])


def matmul_rows(x, w, *, tm):
    m, k = x.shape
    n = w.shape[1]
    return pl.pallas_call(
        _matmul_kernel,
        grid=(m // tm,),
        in_specs=[pl.BlockSpec((tm, k), lambda i: (i, 0)), pl.BlockSpec((k, n), lambda i: (0, 0))],
        out_specs=pl.BlockSpec((tm, n), lambda i: (i, 0)),
        out_shape=jax.ShapeDtypeStruct((m, n), F32),
        compiler_params=_params("parallel"),
        name="matmul_rows",
    )(x, w)


def _cumsum_lanes_kernel(x_ref, o_ref, r_ref, carry_ref, *, tb):
    @pl.when(pl.program_id(1) == 0)
    def _():
        carry_ref[...] = jnp.zeros_like(carry_ref)

    h = x_ref.shape[1]
    r = lax.broadcasted_iota(jnp.int32, (tb, tb), 0)
    c = lax.broadcasted_iota(jnp.int32, (tb, tb), 1)
    upper = (r <= c).astype(F32)
    out = _dot_hi(x_ref[0], upper) + carry_ref[...]
    carry_ref[...] = out[:, tb - 1:tb]
    r_ref[0] = out * LOG2E
    cols = jnp.concatenate([out * LOG2E, jnp.zeros((LANES - h, tb), F32)], axis=0).T
    for i in range(h):
        o_ref[0, i] = cols[:, i:i + 1]


def cumsum_lanes(x, *, tb):
    b, h, t = x.shape
    return pl.pallas_call(
        functools.partial(_cumsum_lanes_kernel, tb=tb),
        grid=(b, t // tb),
        in_specs=[pl.BlockSpec((1, h, tb), lambda i, j: (i, 0, j))],
        out_specs=[pl.BlockSpec((1, h, tb, 1), lambda i, j: (i, 0, j, 0)),
                   pl.BlockSpec((1, h, tb), lambda i, j: (i, 0, j))],
        out_shape=[jax.ShapeDtypeStruct((b, h, t, 1), F32), jax.ShapeDtypeStruct((b, h, t), F32)],
        scratch_shapes=[pltpu.VMEM((h, 1), F32)],
        compiler_params=_params("parallel", "arbitrary"),
        name="cumsum_lanes",
    )(x)


def _fox_prompt_kernel(fast_ref, qt_ref, k_ref, vt_ref, c_ref, cr_ref, qn_ref, kn_ref, o_ref, km_ref, *, tq):
    i = pl.program_id(2)
    plan = fast_ref[(pl.program_id(0) * pl.num_programs(1) + pl.program_id(1)) * pl.num_programs(2) + i]
    fast = plan & 1
    first = plan >> 1

    @pl.when(i == 0)
    def _():
        kb = k_ref[0]
        low = lax.broadcasted_iota(jnp.int32, kb.shape, 1) < A_HEAD_DIM
        zero = jnp.zeros_like(kb)
        km_ref[0] = jnp.where(low, kb, zero)
        km_ref[1] = jnp.where(low, zero, kb)

    qt = qt_ref[0]
    key = lax.broadcasted_iota(jnp.int32, (tq, tq), 0)
    qry = lax.broadcasted_iota(jnp.int32, (tq, tq), 1)
    causal = key <= qry

    def scores_of(j, masked):
        off = pl.multiple_of(j * tq, tq)
        out = []
        for hh in range(2):
            s = _dot(km_ref[hh, pl.ds(off, tq), :], qt) - c_ref[0, hh, pl.ds(off, tq), :]
            out.append(jnp.where(causal, s, NEG_BIG) if masked else s)
        return out, [vt_ref[0, hh * A_HEAD_DIM:(hh + 1) * A_HEAD_DIM, pl.ds(off, tq)] for hh in range(2)]

    def finish(a0, l0, a1, l1):
        o_ref[0] = jnp.concatenate([a0 / l0, a1 / l1], axis=0).T.astype(o_ref.dtype)

    @pl.when(fast != 0)
    def _():
        ref = [qn_ref[0, 0, hh:hh + 1, :] * kn_ref[0, 0, hh:hh + 1, 0:1] - cr_ref[0, 0, hh:hh + 1, :]
               for hh in range(2)]

        def step(j, carry, masked):
            scores, vt = scores_of(j, masked)
            new = []
            for hh in range(2):
                l, acc = carry[hh]
                p = jnp.exp2(scores[hh] - ref[hh])
                new.append((l + jnp.sum(p, axis=0, keepdims=True), acc + _dot(vt[hh], p.astype(BF16))))
            return tuple(new)

        one = (jnp.zeros((1, tq), F32), jnp.zeros((A_HEAD_DIM, tq), F32))
        carry = lax.fori_loop(first, i, lambda j, c: step(j, c, False), (one, one))
        (l0, a0), (l1, a1) = step(i, carry, True)
        finish(a0, l0, a1, l1)

    @pl.when(fast == 0)
    def _():
        def step(j, carry, masked):
            scores, vt = scores_of(j, masked)
            new = []
            for hh in range(2):
                m, l, acc = carry[hh]
                m_new = jnp.maximum(m, jnp.max(scores[hh], axis=0, keepdims=True))
                alpha = jnp.exp2(m - m_new)
                p = jnp.exp2(scores[hh] - m_new)
                new.append((m_new, alpha * l + jnp.sum(p, axis=0, keepdims=True),
                            alpha * acc + _dot(vt[hh], p.astype(BF16))))
            return tuple(new)

        one = (jnp.full((1, tq), NEG_BIG, F32), jnp.zeros((1, tq), F32), jnp.zeros((A_HEAD_DIM, tq), F32))
        carry = lax.fori_loop(0, i, lambda j, c: step(j, c, False), (one, one))
        (_, l0, a0), (_, l1, a1) = step(i, carry, True)
        finish(a0, l0, a1, l1)


FOX_FAST_BOUND = 50.0
FOX_ZERO_EXP2 = -152.0


def fox_prompt(qt, k, vt, c_col, c_row, qn2, kn2, *, tq):
    b, t, _ = k.shape
    pairs = A_HEADS // 2
    nq = t // tq
    qn = jnp.sqrt(qn2)
    kn = jnp.sqrt(jnp.max(kn2, axis=-1)) * 1.01 + 1e-6
    fast = jnp.max((qn * kn[..., None]).reshape(b, pairs, 2, nq, tq), axis=(2, 4)) <= FOX_FAST_BOUND
    c_blocks = c_row.reshape(b, pairs, 2, nq, tq)
    gap = c_blocks[..., None, :, tq - 1] - c_blocks[..., :, None, 0]
    dead = jnp.all(gap >= -FOX_ZERO_EXP2, axis=2) & (jnp.arange(nq)[None, :] < jnp.arange(nq)[:, None])
    first = jnp.sum(jnp.cumprod(dead.astype(jnp.int32), axis=-1), axis=-1)
    plan = fast.astype(jnp.int32) + 2 * jnp.where(fast, first, 0)
    kn_l = jnp.broadcast_to(kn.reshape(b, pairs, 2, 1), (b, pairs, 2, LANES))
    pair_row = lambda bi, hp, i, f: (bi, hp, 0, i)
    grid_spec = pltpu.PrefetchScalarGridSpec(
        num_scalar_prefetch=1,
        grid=(b, pairs, nq),
        in_specs=[
            pl.BlockSpec((1, LANES, tq), lambda bi, hp, i, f: (bi, hp, i)),
            pl.BlockSpec((1, t, LANES), lambda bi, hp, i, f: (bi, 0, hp)),
            pl.BlockSpec((1, LANES, t), lambda bi, hp, i, f: (bi, hp, 0)),
            pl.BlockSpec((1, 2, t, 1), lambda bi, hp, i, f: (bi, hp, 0, 0)),
            pl.BlockSpec((1, 1, 2, tq), pair_row),
            pl.BlockSpec((1, 1, 2, tq), pair_row),
            pl.BlockSpec((1, 1, 2, LANES), lambda bi, hp, i, f: (bi, hp, 0, 0)),
        ],
        out_specs=pl.BlockSpec((1, tq, LANES), lambda bi, hp, i, f: (bi, i, hp)),
        scratch_shapes=[pltpu.VMEM((2, t, LANES), BF16)],
    )
    return pl.pallas_call(
        functools.partial(_fox_prompt_kernel, tq=tq),
        grid_spec=grid_spec,
        out_shape=jax.ShapeDtypeStruct((b, t, A_WIDTH), BF16),
        compiler_params=_params("parallel", "parallel", "arbitrary"),
        name="fox_prompt",
    )(plan.reshape(-1), qt, k, vt, c_col, c_row.reshape(b, pairs, 2, t),
      qn.reshape(b, pairs, 2, t), kn_l)


def _fox_sample_kernel(pt_ref, q_ref, kn_ref, vn_ref, lfn_ref, *rest, n_q, group):
    kp_refs, vp_refs, lfp_refs = rest[:group], rest[group:2 * group], rest[2 * group:3 * group]
    o_ref, qbd_ref, m_ref, l_ref, acc_ref, carry_ref = rest[3 * group:]
    p = pl.program_id(1)
    rows = n_q * A_HEADS
    hrow = lax.broadcasted_iota(jnp.int32, (A_HEADS, A_WIDTH), 0)
    hcol = lax.broadcasted_iota(jnp.int32, (A_HEADS, A_WIDTH), 1) // A_HEAD_DIM
    head_mask = hrow == hcol

    @pl.when(p == 0)
    def _():
        m_ref[...] = jnp.full_like(m_ref, NEG_BIG)
        l_ref[...] = jnp.zeros_like(l_ref)
        acc_ref[...] = jnp.zeros_like(acc_ref)
        carry_ref[...] = jnp.zeros_like(carry_ref)
        for qi in range(n_q):
            qrow = jnp.broadcast_to(q_ref[0, qi:qi + 1, :], (A_HEADS, A_WIDTH))
            qbd_ref[qi * A_HEADS:(qi + 1) * A_HEADS, :] = jnp.where(head_mask, qrow, 0.0)

    qbd = qbd_ref[...].astype(BF16)
    r = lax.broadcasted_iota(jnp.int32, (PAGE_SIZE, PAGE_SIZE), 0)
    c = lax.broadcasted_iota(jnp.int32, (PAGE_SIZE, PAGE_SIZE), 1)
    after = (r > c).astype(F32)
    carry = carry_ref[...]
    scores = []
    for g in range(group):
        lf = lfp_refs[g][...]
        bias = _dot_hi(lf, after) + carry
        carry = carry + jnp.sum(lf, axis=-1, keepdims=True)
        s = _dot(qbd, kp_refs[g][...].astype(BF16))
        scores.append(s + jnp.concatenate([bias] * n_q, axis=0))
    carry_ref[...] = carry
    m_old = m_ref[...]
    m_new = m_old
    for s in scores:
        m_new = jnp.maximum(m_new, jnp.max(s, axis=-1, keepdims=True))
    alpha = jnp.exp(m_old - m_new)
    l_new = alpha * l_ref[...]
    pv = None
    for g, s in enumerate(scores):
        pr = jnp.exp(s - m_new)
        l_new = l_new + jnp.sum(pr, axis=-1, keepdims=True)
        t = _dot_nt(pr.astype(BF16), vp_refs[g][...].astype(BF16))
        pv = t if pv is None else pv + t
    l_ref[...] = l_new
    acc_ref[...] = alpha * acc_ref[...] + pv
    m_ref[...] = m_new

    @pl.when(p == pl.num_programs(1) - 1)
    def _():
        qbd = qbd_ref[...]
        lfn = lfn_ref[0]
        qidx = lax.broadcasted_iota(jnp.int32, (rows, 1), 0) // A_HEADS
        s_new = []
        cum = jnp.zeros((A_HEADS, 1), F32)
        for ki in range(n_q):
            cum = cum + lfn[:, ki:ki + 1]
            sk = jnp.sum(qbd * kn_ref[0, ki:ki + 1, :], axis=-1, keepdims=True)
            sk = sk - jnp.concatenate([cum] * n_q, axis=0)
            s_new.append(jnp.where(qidx >= ki, sk, NEG_BIG))
        m_old = m_ref[...]
        m_fin = m_old
        for sk in s_new:
            m_fin = jnp.maximum(m_fin, sk)
        alpha = jnp.exp(m_old - m_fin)
        l_fin = alpha * l_ref[...]
        acc = alpha * acc_ref[...]
        for ki, sk in enumerate(s_new):
            pk = jnp.exp(sk - m_fin)
            l_fin = l_fin + pk
            acc = acc + pk * vn_ref[0, ki:ki + 1, :]
        acc = acc / l_fin
        for qi in range(n_q):
            blk = jnp.where(head_mask, acc[qi * A_HEADS:(qi + 1) * A_HEADS, :], 0.0)
            o_ref[0, qi:qi + 1, :] = jnp.sum(blk, axis=0, keepdims=True)


def fox_sample(page_table, q, k_new, v_new, lf_new_t, cache_kt, cache_vt, cache_lft, *, layer, group):
    bd, n_q, _ = q.shape
    n_pages = page_table.shape[1]
    rows = n_q * A_HEADS
    pt = page_table.reshape(-1)
    assert n_pages % group == 0

    def page(g):
        return lambda b, p, pt_ref: (layer, pt_ref[b * n_pages + (n_pages - 1 - (p * group + g))], 0, 0)

    seq3 = lambda b, p, pt_ref: (b, 0, 0)
    grid_spec = pltpu.PrefetchScalarGridSpec(
        num_scalar_prefetch=1,
        grid=(bd, n_pages // group),
        in_specs=[
            pl.BlockSpec((1, n_q, A_WIDTH), seq3),
            pl.BlockSpec((1, n_q, A_WIDTH), seq3),
            pl.BlockSpec((1, n_q, A_WIDTH), seq3),
            pl.BlockSpec((1, A_HEADS, n_q), seq3),
        ] + [pl.BlockSpec((None, None, A_WIDTH, PAGE_SIZE), page(g)) for g in range(group)] * 2
          + [pl.BlockSpec((None, None, A_HEADS, PAGE_SIZE), page(g)) for g in range(group)],
        out_specs=pl.BlockSpec((1, n_q, A_WIDTH), seq3),
        scratch_shapes=[
            pltpu.VMEM((rows, A_WIDTH), F32),
            pltpu.VMEM((rows, 1), F32),
            pltpu.VMEM((rows, 1), F32),
            pltpu.VMEM((rows, A_WIDTH), F32),
            pltpu.VMEM((A_HEADS, 1), F32),
        ],
    )
    return pl.pallas_call(
        functools.partial(_fox_sample_kernel, n_q=n_q, group=group),
        grid_spec=grid_spec,
        out_shape=jax.ShapeDtypeStruct((bd, n_q, A_WIDTH), F32),
        compiler_params=_params("parallel", "arbitrary"),
        name="fox_sample",
    )(pt, q, k_new, v_new, lf_new_t, *([cache_kt] * group + [cache_vt] * group + [cache_lft] * group))


def _pool_kernel(u_ref, pre_ref, w_ref, sc_ref, o_ref, ext_ref, *, tt, n_valid):
    halo = POOL_BUF + 1
    j = pl.program_id(1)

    @pl.when(j == 0)
    def _():
        ext_ref[0:halo, :] = pre_ref[0]

    @pl.when(j > 0)
    def _():
        ext_ref[0:halo, :] = ext_ref[tt:tt + halo, :]

    u = u_ref[0]
    ext_ref[halo:halo + tt, :] = u
    pos = j * tt + lax.broadcasted_iota(jnp.int32, (tt, 1), 0) + (1 + n_valid)
    outs = []
    for gi, w in enumerate(POOL_WINDOWS):
        sl = slice(gi * POOL_GROUP_DIM, (gi + 1) * POOL_GROUP_DIM)
        tot = u[:, sl]
        for k in range(1, w):
            tot = tot + ext_ref[halo - k:halo - k + tt, sl]
        cnt = jnp.minimum(pos, w).astype(F32)
        d = tot / cnt - u[:, sl]
        outs.append(_dot(d.astype(BF16), w_ref[gi]))
    o_ref[0] = (jnp.concatenate(outs, axis=-1) * sc_ref[...]).astype(o_ref.dtype)


def pool_mix(u, prefix16, pool_w, pool_scale, *, tt, n_valid, out_dtype):
    b, t, _ = u.shape
    halo = POOL_BUF + 1
    return pl.pallas_call(
        functools.partial(_pool_kernel, tt=tt, n_valid=n_valid),
        grid=(b, t // tt),
        in_specs=[
            pl.BlockSpec((1, tt, B_WIDTH), lambda i, j: (i, j, 0)),
            pl.BlockSpec((1, halo, B_WIDTH), lambda i, j: (i, 0, 0)),
            pl.BlockSpec((len(POOL_WINDOWS), POOL_GROUP_DIM, POOL_GROUP_DIM), lambda i, j: (0, 0, 0)),
            pl.BlockSpec((1, B_WIDTH), lambda i, j: (0, 0)),
        ],
        out_specs=pl.BlockSpec((1, tt, B_WIDTH), lambda i, j: (i, j, 0)),
        out_shape=jax.ShapeDtypeStruct((b, t, B_WIDTH), out_dtype),
        scratch_shapes=[pltpu.VMEM((tt + 2 * halo, B_WIDTH), F32)],
        compiler_params=_params("parallel", "arbitrary"),
        name="pool_mix",
    )(u, prefix16, pool_w, pool_scale)


def _gdn_kernel(x_ref, ab_ref, z_ref, pre_ref, s0_ref, cw_ref, alog_ref, dtb_ref, nw_ref,
                o_ref, s_out_ref, ext_ref, s_ref, *, chunk, n_chunks, t_valid):
    j = pl.program_id(1)
    halo = 8

    @pl.when(j == 0)
    def _():
        ext_ref[0:halo, :] = pre_ref[0]
        s_ref[...] = s0_ref[0]

    span = chunk * n_chunks

    @pl.when(j > 0)
    def _():
        ext_ref[0:halo, :] = ext_ref[span:span + halo, :]

    ext_ref[halo:halo + span, :] = x_ref[0]
    y_all = ext_ref[halo:halo + span, :] * cw_ref[CONV_WIDTH - 1:CONV_WIDTH, :]
    for k in range(1, CONV_WIDTH):
        y_all = y_all + ext_ref[halo - k:halo - k + span, :] * cw_ref[CONV_WIDTH - 1 - k:CONV_WIDTH - k, :]
    y_all = _silu(y_all)

    ab = ab_ref[0]
    valid = (j * span + lax.broadcasted_iota(jnp.int32, (span, 1), 0)) < t_valid
    g_all = jnp.where(valid, -jnp.exp(alog_ref[...]) * _softplus(ab + dtb_ref[...]), 0.0)
    beta_all = jnp.where(valid, _sigmoid(ab), 0.0)
    z_all = z_ref[0]
    r = lax.broadcasted_iota(jnp.int32, (chunk, chunk), 0)
    c = lax.broadcasted_iota(jnp.int32, (chunk, chunk), 1)
    tri = (r >= c).astype(F32)

    n = C_HEADS * chunk
    rows = [slice(h * chunk, (h + 1) * chunk) for h in range(C_HEADS)]
    row_head = lax.broadcasted_iota(jnp.int32, (n, LANES), 0) // chunk
    lane = lax.broadcasted_iota(jnp.int32, (n, LANES), 1)
    pick = lambda a, off: jnp.where(lane == row_head + off, jnp.concatenate([a] * C_HEADS, axis=0), 0.0)
    rr = lax.broadcasted_iota(jnp.int32, (n, n), 0)
    cc = lax.broadcasted_iota(jnp.int32, (n, n), 1)
    same = (rr // chunk) == (cc // chunk)

    def setup(ci):
        tok = slice(ci * chunk, (ci + 1) * chunk)
        y = y_all[tok]
        gcum = _dot_hi(tri, g_all[tok])
        heads = lambda off: jnp.concatenate(
            [y[:, off + h * C_HEAD_DIM:off + (h + 1) * C_HEAD_DIM] for h in range(C_HEADS)], axis=0)
        qs, ks, vs = heads(0), heads(C_WIDTH), heads(2 * C_WIDTH)
        qs = qs * lax.rsqrt(jnp.sum(qs * qs, axis=-1, keepdims=True) + L2_EPS) * (C_HEAD_DIM ** -0.5)
        ks = ks * lax.rsqrt(jnp.sum(ks * ks, axis=-1, keepdims=True) + L2_EPS)
        gd = pick(gcum, 0)
        gc = jnp.sum(gd, axis=-1, keepdims=True)
        gr = _dot_nt(jnp.ones((8, LANES), F32), gd, precision=HIGHEST)[0:1, :]
        beta = jnp.sum(pick(beta_all[tok], C_HEADS), axis=-1, keepdims=True)
        g_last = jnp.sum(pick(jnp.broadcast_to(gcum[chunk - 1:chunk, :], (chunk, LANES)), 0),
                         axis=-1, keepdims=True)
        gamma = jnp.exp(jnp.where(same & (rr >= cc), gc - gr, -jnp.inf))
        eg = jnp.exp(gc)
        ksb = ks.astype(BF16)
        return dict(
            nmat=jnp.where(same & (rr > cc), beta * _dot_nt(ksb, ksb) * gamma, 0.0),
            bv=_split(beta * vs), bk=_split(beta * ks * eg),
            qk=(_dot_nt(qs.astype(BF16), ksb) * gamma).astype(BF16),
            k_dec=ks * jnp.exp(g_last - gc), q_dec=qs * eg,
            decay=[jnp.exp(gcum[chunk - 1:chunk, h:h + 1]) for h in range(C_HEADS)],
            z=jnp.concatenate([z_all[tok, h * C_HEAD_DIM:(h + 1) * C_HEAD_DIM] for h in range(C_HEADS)], axis=0))

    parts = [setup(ci) for ci in range(n_chunks)]
    inv = [(rr == cc).astype(F32) - p["nmat"] for p in parts]
    pw = [_split(p["nmat"]) for p in parts]
    for _ in range(max(1, (chunk - 1).bit_length()) - 1):
        pw = [_split(_dot(x[0], x[0])) for x in pw]
        inv = [a + _mm_split(_split(a), x) for a, x in zip(inv, pw)]
    inv = [_split(a) for a in inv]
    for p, a in zip(parts, inv):
        p["u"] = _mm_split(a, p["bv"])
        p["w"] = _mm_split(a, p["bk"])
    state = [s_ref[h] for h in range(C_HEADS)]
    for ci, p in enumerate(parts):
        state_b = [s.astype(BF16) for s in state]
        v_new = jnp.concatenate(
            [p["u"][rows[h]] - _dot(p["w"][rows[h]].astype(BF16), state_b[h]) for h in range(C_HEADS)], axis=0)
        o = jnp.concatenate([_dot(p["q_dec"][rows[h]].astype(BF16), state_b[h]) for h in range(C_HEADS)], axis=0)
        o = o + _dot(p["qk"], v_new.astype(BF16))
        state = [state[h] * p["decay"][h] + lax.dot_general(
            p["k_dec"][rows[h]].astype(BF16), v_new[rows[h]].astype(BF16), (((0,), (0,)), ((), ())),
            preferred_element_type=F32) for h in range(C_HEADS)]
        o = o * lax.rsqrt(jnp.mean(o * o, axis=-1, keepdims=True) + RMS_EPS) * nw_ref[...]
        o = o * _silu(p["z"])
        o_ref[0, ci * chunk:(ci + 1) * chunk, :] = jnp.concatenate(
            [o[rows[h]] for h in range(C_HEADS)], axis=-1).astype(o_ref.dtype)
    for h in range(C_HEADS):
        s_ref[h] = state[h]

    @pl.when(j == pl.num_programs(1) - 1)
    def _():
        s_out_ref[0] = s_ref[...]


def gdn_mixer(x, ab, z, prefix8, s0, conv_w, a_log_row, dt_bias_row, norm_w, *, chunk, n_chunks, t_valid,
              out_dtype):
    b, t, _ = x.shape
    span = chunk * n_chunks
    row = lambda i, j: (0, 0)
    tok = lambda i, j: (i, j, 0)
    seq = lambda i, j: (i, 0, 0)
    return pl.pallas_call(
        functools.partial(_gdn_kernel, chunk=chunk, n_chunks=n_chunks, t_valid=t_valid),
        grid=(b, t // span),
        in_specs=[
            pl.BlockSpec((1, span, 3 * C_WIDTH), tok),
            pl.BlockSpec((1, span, LANES), tok),
            pl.BlockSpec((1, span, C_WIDTH), tok),
            pl.BlockSpec((1, 8, 3 * C_WIDTH), seq),
            pl.BlockSpec((1, C_HEADS, C_HEAD_DIM, C_HEAD_DIM), lambda i, j: (i, 0, 0, 0)),
            pl.BlockSpec((CONV_WIDTH, 3 * C_WIDTH), row),
            pl.BlockSpec((1, LANES), row),
            pl.BlockSpec((1, LANES), row),
            pl.BlockSpec((1, C_HEAD_DIM), row),
        ],
        out_specs=[
            pl.BlockSpec((1, span, C_WIDTH), tok),
            pl.BlockSpec((1, C_HEADS, C_HEAD_DIM, C_HEAD_DIM), lambda i, j: (i, 0, 0, 0)),
        ],
        out_shape=[
            jax.ShapeDtypeStruct((b, t, C_WIDTH), out_dtype),
            jax.ShapeDtypeStruct((b, C_HEADS, C_HEAD_DIM, C_HEAD_DIM), F32),
        ],
        scratch_shapes=[
            pltpu.VMEM((span + 16, 3 * C_WIDTH), F32),
            pltpu.VMEM((C_HEADS, C_HEAD_DIM, C_HEAD_DIM), F32),
        ],
        compiler_params=_params("parallel", "arbitrary"),
        name="gdn_mixer",
    )(x, ab, z, prefix8, s0, conv_w, a_log_row, dt_bias_row, norm_w)


def _merge_kernel(h_ref, oa_ref, ob_ref, oc_ref, wg_ref, wa_ref, wb_ref, wc_ref, wo_ref, g_ref, b_ref, o_ref):
    h = h_ref[...]
    hb = h.astype(BF16)
    merged = None
    for i, (x_ref, w_ref) in enumerate(((oa_ref, wa_ref), (ob_ref, wb_ref), (oc_ref, wc_ref))):
        gate = _sigmoid(_dot(hb, wg_ref[:, i * D_MODEL:(i + 1) * D_MODEL]))
        term = gate * _dot(x_ref[...].astype(BF16), w_ref[...])
        merged = term if merged is None else merged + term
    y = DN_ALPHA * h + _dot(merged.astype(BF16), wo_ref[...])
    o_ref[...] = _layer_norm(y, g_ref[...], b_ref[...])


def merge_ln(h, oa, ob, oc, wg, wa, wb, wc, wo, g, b, *, tm):
    m = h.shape[0]
    const = lambda i: (0, 0)
    rows = lambda i: (i, 0)
    return pl.pallas_call(
        _merge_kernel,
        grid=(m // tm,),
        in_specs=[
            pl.BlockSpec((tm, D_MODEL), rows),
            pl.BlockSpec((tm, A_WIDTH), rows),
            pl.BlockSpec((tm, B_WIDTH), rows),
            pl.BlockSpec((tm, C_WIDTH), rows),
            pl.BlockSpec((D_MODEL, N_BRANCH * D_MODEL), const),
            pl.BlockSpec((A_WIDTH, D_MODEL), const),
            pl.BlockSpec((B_WIDTH, D_MODEL), const),
            pl.BlockSpec((C_WIDTH, D_MODEL), const),
            pl.BlockSpec((D_MODEL, D_MODEL), const),
            pl.BlockSpec((1, D_MODEL), const),
            pl.BlockSpec((1, D_MODEL), const),
        ],
        out_specs=pl.BlockSpec((tm, D_MODEL), rows),
        out_shape=jax.ShapeDtypeStruct((m, D_MODEL), F32),
        compiler_params=_params("parallel"),
        name="merge_ln",
    )(h, oa, ob, oc, wg, wa, wb, wc, wo, g, b)


def _split_w_in(w_in_l):
    o = 0
    parts = {}
    for name, n in (("q", A_WIDTH), ("k", A_WIDTH), ("v", A_WIDTH), ("f", A_HEADS), ("ub", B_WIDTH),
                    ("qkvc", 3 * C_WIDTH), ("a", C_HEADS), ("b", C_HEADS), ("z", C_WIDTH),
                    ("gate", N_BRANCH * D_MODEL)):
        parts[name] = w_in_l[:, o:o + n]
        o += n
    return parts


def kernel(x_prompt, x_sample, cache_k, cache_v, cache_logf, page_table, state_pool, state_conv, state_ssm,
           w_in, fox_f_bias, gdn_conv_w, gdn_a_log, gdn_dt_bias, gdn_norm_w, pool_w, pool_scale,
           w_branch_a, w_branch_b, w_branch_c, w_out, ffn_w_in, ffn_w_out, ln_g, ln_b):
    bp, seq, _ = x_prompt.shape
    bd, n_q, _ = x_sample.shape
    mp, ms = bp * seq, bd * n_q
    tm_p = 512
    tm_s = ms

    n_pool = cache_k.shape[1]
    cache_kt = cache_k.transpose(0, 1, 3, 4, 2).reshape(DEPTH, n_pool, A_WIDTH, PAGE_SIZE)
    cache_vt = cache_v.transpose(0, 1, 3, 4, 2).reshape(DEPTH, n_pool, A_WIDTH, PAGE_SIZE)
    cache_lft = cache_logf.transpose(0, 1, 3, 2)

    yp = x_prompt.reshape(mp, D_MODEL)
    ys = x_sample.reshape(ms, D_MODEL)
    outs_p = ([], [], [], [], [], [])
    outs_s = ([], [], [], [], [], [])
    zeros_ab = jnp.zeros((D_MODEL, LANES - 2 * C_HEADS), F32)
    for l in range(DEPTH):
        wp = _split_w_in(w_in[l])
        wq = (wp["q"] * (A_HEAD_DIM ** -0.5)).astype(BF16)
        w_ab = jnp.concatenate([wp["a"], wp["b"], zeros_ab], axis=1)
        w_rest = jnp.concatenate([wp["qkvc"], wp["ub"], wp["z"], w_ab], axis=1).astype(BF16)
        o_qkvc, o_ub, o_z, o_ab = 0, 3 * C_WIDTH, 3 * C_WIDTH + B_WIDTH, 3 * C_WIDTH + B_WIDTH + C_WIDTH
        w_f_pad = jnp.concatenate([wp["f"], jnp.zeros((D_MODEL, LANES - A_HEADS), F32)], axis=1)
        w_sample = jnp.concatenate([wq.astype(F32), wp["k"], wp["v"], w_f_pad], axis=1).astype(BF16)
        w_gate = wp["gate"].astype(BF16)
        ffn_up = ffn_w_in[l].astype(BF16)
        ffn_dn = ffn_w_out[l].astype(BF16)
        lg = ln_g[l].reshape(3, 1, D_MODEL)
        lb = ln_b[l].reshape(3, 1, D_MODEL)
        fb = fox_f_bias[l]
        a_log_row = jnp.zeros((1, LANES), F32).at[0, :C_HEADS].set(gdn_a_log[l])
        dt_row = jnp.zeros((1, LANES), F32).at[0, :C_HEADS].set(gdn_dt_bias[l])
        norm_w = gdn_norm_w[l].reshape(1, C_HEAD_DIM)
        pw = pool_w[l].astype(BF16)
        psc = pool_scale[l].reshape(1, B_WIDTH)
        w_merge = (w_gate, w_branch_a[l].astype(BF16), w_branch_b[l].astype(BF16),
                   w_branch_c[l].astype(BF16), w_out[l].astype(BF16))

        hp = ffn_ln(yp, ffn_up[0], ffn_dn[0], lg[0], lb[0], tm=tm_p)
        hs = ffn_ln(ys, ffn_up[0], ffn_dn[0], lg[0], lb[0], tm=tm_s)

        wqt = (wp["q"] * (A_HEAD_DIM ** -0.5 * LOG2E)).T.astype(BF16)
        qt_p, kb_p, kt_p, vt_p, vtb_p, lft_p, qn2_p, kn2_p, qkvc_p, ub_p, z_p, ab_p = inproj_prompt(
            hp, wqt, wp["k"].astype(BF16), wp["k"].T.astype(BF16), wp["v"].T.astype(BF16), wp["f"].T.astype(BF16),
            fb.reshape(A_HEADS, 1), wp["qkvc"].astype(BF16), wp["ub"].astype(BF16), wp["z"].astype(BF16),
            w_ab.astype(BF16), batch=bp, seq=seq, tm=256)
        ccol_p, crow_p = cumsum_lanes(lft_p, tb=512)
        oa_p = fox_prompt(qt_p, kb_p, vtb_p, ccol_p, crow_p, qn2_p, kn2_p, tq=1024).reshape(mp, A_WIDTH)
        ob_p = pool_mix(ub_p, jnp.zeros((bp, POOL_BUF + 1, B_WIDTH), F32), pw, psc,
                        tt=512, n_valid=0, out_dtype=BF16).reshape(mp, B_WIDTH)
        oc_p, ssm_p = gdn_mixer(qkvc_p, ab_p, z_p,
                                jnp.zeros((bp, 8, 3 * C_WIDTH), F32),
                                jnp.zeros((bp, C_HEADS, C_HEAD_DIM, C_HEAD_DIM), F32),
                                gdn_conv_w[l], a_log_row, dt_row, norm_w, chunk=64, n_chunks=4, t_valid=seq,
                                out_dtype=BF16)
        yp = merge_ln(hp, oa_p, ob_p, oc_p.reshape(mp, C_WIDTH), *w_merge, lg[1], lb[1], tm=tm_p)
        yp = ffn_ln(yp, ffn_up[1], ffn_dn[1], lg[2], lb[2], tm=tm_p)
        k_p = kt_p.reshape(bp, A_HEADS, A_HEAD_DIM, seq).transpose(0, 3, 1, 2)
        v_p = vt_p.reshape(bp, A_HEADS, A_HEAD_DIM, seq).transpose(0, 3, 1, 2)
        for lst, a in zip(outs_p, (k_p, v_p, lft_p.transpose(0, 2, 1), ub_p[:, seq - POOL_BUF:],
                                   qkvc_p[:, seq - CONV_BUF:], ssm_p)):
            lst.append(a)

        proj_s = matmul_rows(hs, w_sample, tm=tm_s)
        rest_s = matmul_rows(hs, w_rest, tm=tm_s).reshape(bd, n_q, -1)
        q_s = proj_s[:, :A_WIDTH].reshape(bd, n_q, A_WIDTH)
        k_s = proj_s[:, A_WIDTH:2 * A_WIDTH].reshape(bd, n_q, A_WIDTH)
        v_s = proj_s[:, 2 * A_WIDTH:3 * A_WIDTH].reshape(bd, n_q, A_WIDTH)
        lf_s = jax.nn.log_sigmoid(proj_s[:, 3 * A_WIDTH:3 * A_WIDTH + A_HEADS] + fb).reshape(bd, n_q, A_HEADS)
        oa_s = fox_sample(page_table, q_s, k_s, v_s, lf_s.transpose(0, 2, 1), cache_kt, cache_vt, cache_lft,
                          layer=l, group=8).reshape(ms, A_WIDTH)
        ub_s = rest_s[:, :, o_ub:o_ub + B_WIDTH]
        qkvc_s = rest_s[:, :, o_qkvc:o_qkvc + 3 * C_WIDTH]
        pool_full = jnp.concatenate([state_pool[l], ub_s], axis=1)
        conv_full = jnp.concatenate([state_conv[l], qkvc_s], axis=1)
        pre16 = jnp.concatenate([jnp.zeros((bd, 1, B_WIDTH), F32), state_pool[l]], axis=1)
        ob_s = pool_mix(ub_s, pre16, pw, psc, tt=n_q, n_valid=POOL_BUF, out_dtype=F32).reshape(ms, B_WIDTH)
        pad_t = lambda a: jnp.pad(a, ((0, 0), (0, 8 - n_q), (0, 0)))
        pre8 = jnp.concatenate([jnp.zeros((bd, 8 - CONV_BUF, 3 * C_WIDTH), F32), state_conv[l]], axis=1)
        oc_s, ssm_s = gdn_mixer(pad_t(qkvc_s), pad_t(rest_s[:, :, o_ab:o_ab + LANES]),
                                pad_t(rest_s[:, :, o_z:o_z + C_WIDTH]), pre8, state_ssm[l],
                                gdn_conv_w[l], a_log_row, dt_row, norm_w, chunk=8, n_chunks=1, t_valid=n_q,
                                out_dtype=F32)
        ys = merge_ln(hs, oa_s, ob_s, oc_s[:, :n_q].reshape(ms, C_WIDTH), *w_merge, lg[1], lb[1], tm=tm_s)
        ys = ffn_ln(ys, ffn_up[1], ffn_dn[1], lg[2], lb[2], tm=tm_s)
        for lst, a in zip(outs_s, (k_s.reshape(bd, n_q, A_HEADS, A_HEAD_DIM), v_s.reshape(bd, n_q, A_HEADS, A_HEAD_DIM),
                                   lf_s, pool_full[:, -POOL_BUF:], conv_full[:, -CONV_BUF:], ssm_s)):
            lst.append(a)

    k_p, v_p, lf_p, pool_p, conv_p, ssm_p = [jnp.stack(a) for a in outs_p]
    k_s, v_s, lf_s, pool_s, conv_s, ssm_s = [jnp.stack(a) for a in outs_s]
    return (yp.reshape(bp, seq, D_MODEL), ys.reshape(bd, n_q, D_MODEL), k_p, v_p, lf_p, pool_p, conv_p, ssm_p,
            k_s, v_s, lf_s, pool_s, conv_s, ssm_s)
```

```python
import functools

import jax
import jax.numpy as jnp
from jax import lax
from jax.experimental import pallas as pl
from jax.experimental.pallas import tpu as pltpu

F32 = jnp.float32
BF16 = jnp.bfloat16
HIGHEST = lax.Precision.HIGHEST

D_MODEL = 1024
DEPTH = 2
PAGE_SIZE = 128
A_HEADS = 16
A_HEAD_DIM = 64
A_WIDTH = A_HEADS * A_HEAD_DIM
POOL_WINDOWS = (2, 4, 8, 16)
POOL_GROUP_DIM = 128
B_WIDTH = len(POOL_WINDOWS) * POOL_GROUP_DIM
POOL_BUF = max(POOL_WINDOWS) - 1
C_HEADS = 4
C_HEAD_DIM = 128
C_WIDTH = C_HEADS * C_HEAD_DIM
CONV_WIDTH = 4
CONV_BUF = CONV_WIDTH - 1
N_BRANCH = 3
D_FF = ((8 * D_MODEL // 3 + 127) // 128) * 128
DN_ALPHA = (2 * DEPTH) ** 0.25
LN_EPS = 1e-5
RMS_EPS = 1e-6
L2_EPS = 1e-6
NEG_BIG = -1e30
LOG2E = 1.4426950408889634

VMEM_LIMIT_BYTES = 56 * 1024 * 1024
LANES = 128


def _params(*sem):
    return pltpu.CompilerParams(dimension_semantics=sem, vmem_limit_bytes=VMEM_LIMIT_BYTES)


def _layer_norm(y, g, b):
    mu = jnp.mean(y, axis=-1, keepdims=True)
    d = y - mu
    var = jnp.mean(d * d, axis=-1, keepdims=True)
    return d * lax.rsqrt(var + LN_EPS) * g + b


def _log_sigmoid(x):
    return jnp.minimum(x, 0.0) - jnp.log(1.0 + jnp.exp(-jnp.abs(x)))


def _softplus(x):
    return jnp.maximum(x, 0.0) + jnp.log(1.0 + jnp.exp(-jnp.abs(x)))


def _sigmoid(x):
    return 1.0 / (1.0 + jnp.exp(-x))


def _silu(x):
    return x * _sigmoid(x)


def _dot(a, b):
    return jnp.dot(a, b, preferred_element_type=F32)


def _dot_nt(a, b, precision=None):
    return lax.dot_general(a, b, (((1,), (1,)), ((), ())), preferred_element_type=F32, precision=precision)


def _dot_hi(a, b):
    return jnp.dot(a, b, preferred_element_type=F32, precision=HIGHEST)


def _split(a):
    hi = a.astype(BF16)
    return hi, (a - hi.astype(F32)).astype(BF16)


def _mm_split(a, b):
    (ah, al), (bh, bl) = a, b
    return _dot(ah, bh) + (_dot(ah, bl) + _dot(al, bh))


FFN_COLUMN_SPLITS = (0, 768, 1536, 2304, D_FF)


def _ffn_ln_kernel(x_ref, wup_ref, wd_ref, g_ref, b_ref, o_ref):
    x = x_ref[...]
    xb = x.astype(BF16)
    groups = list(zip(FFN_COLUMN_SPLITS[:-1], FFN_COLUMN_SPLITS[1:]))
    gate_up = lambda lo, hi: (_dot(xb, wup_ref[:, lo:hi]), _dot(xb, wup_ref[:, D_FF + lo:D_FF + hi]))
    pending = gate_up(*groups[0])
    acc = None
    for gi, (lo, hi) in enumerate(groups):
        gate, up = pending
        if gi + 1 < len(groups):
            pending = gate_up(*groups[gi + 1])
        part = _dot((_silu(gate) * up).astype(BF16), wd_ref[lo:hi, :])
        acc = part if acc is None else acc + part
    o_ref[...] = _layer_norm(DN_ALPHA * x + 0.5 * acc, g_ref[...], b_ref[...])


def ffn_ln(x, w_up, w_down, g, b, *, tm):
    m = x.shape[0]
    const = lambda i: (0, 0)
    resident = dict(pipeline_mode=pl.Buffered(1))
    return pl.pallas_call(
        _ffn_ln_kernel,
        grid=(m // tm,),
        in_specs=[
            pl.BlockSpec((tm, D_MODEL), lambda i: (i, 0)),
            pl.BlockSpec((D_MODEL, 2 * D_FF), const, **resident),
            pl.BlockSpec((D_FF, D_MODEL), const, **resident),
            pl.BlockSpec((1, D_MODEL), const),
            pl.BlockSpec((1, D_MODEL), const),
        ],
        out_specs=pl.BlockSpec((tm, D_MODEL), lambda i: (i, 0)),
        out_shape=jax.ShapeDtypeStruct((m, D_MODEL), F32),
        compiler_params=_params("parallel"),
        name="ffn_ln",
    )(x, w_up, w_down, g, b)


def _inproj_prompt_kernel(h_ref, wqt_ref, wk_ref, wkt_ref, wvt_ref, wft_ref, fb_ref, wc_ref, wu_ref, wz_ref, wab_ref,
                          qt_ref, kb_ref, kt_ref, vt_ref, vtb_ref, lft_ref, qn_ref, kn_ref, c_ref, u_ref, z_ref, ab_ref):
    hb = h_ref[...].astype(BF16)
    tm = hb.shape[0]
    head_sq = lambda a: jnp.sum((a * a).reshape(A_HEADS, A_HEAD_DIM, tm), axis=1)
    qt = _dot_nt(wqt_ref[...], hb).astype(BF16)
    qt_ref[0] = qt
    qn_ref[0] = head_sq(qt.astype(F32))
    kb_ref[0] = _dot(hb, wk_ref[...]).astype(BF16)
    kt = _dot_nt(wkt_ref[...], hb)
    kt_ref[0] = kt
    kn_ref[0] = head_sq(kt)
    vt = _dot_nt(wvt_ref[...], hb)
    vt_ref[0] = vt
    vtb_ref[0] = vt.astype(BF16)
    lft_ref[0] = _log_sigmoid(_dot_nt(wft_ref[...], hb) + fb_ref[...])
    c_ref[0] = _dot(hb, wc_ref[...])
    u_ref[0] = _dot(hb, wu_ref[...])
    z_ref[0] = _dot(hb, wz_ref[...])
    ab_ref[0] = _dot(hb, wab_ref[...])


def inproj_prompt(h, wqt, wk, wkt, wvt, wft, fbias, wc, wu, wz, wab, *, batch, seq, tm):
    m = h.shape[0]
    nt = seq // tm
    const = lambda i: (0, 0)
    tok3 = lambda i: (i // nt, i % nt, 0)
    feat3 = lambda i: (i // nt, 0, i % nt)
    weights = (wqt, wk, wkt, wvt, wft, fbias, wc, wu, wz, wab)
    tok_outs = ((A_WIDTH, BF16), (3 * C_WIDTH, F32), (B_WIDTH, F32), (C_WIDTH, F32), (LANES, F32))
    feat_outs = ((A_WIDTH, BF16), (A_WIDTH, F32), (A_WIDTH, F32), (A_WIDTH, BF16), (A_HEADS, F32),
                 (A_HEADS, F32), (A_HEADS, F32))
    specs = ([(feat_outs[0], True), (tok_outs[0], False)] + [(f, True) for f in feat_outs[1:]]
             + [(t, False) for t in tok_outs[1:]])
    return pl.pallas_call(
        _inproj_prompt_kernel,
        grid=(m // tm,),
        in_specs=[pl.BlockSpec((tm, D_MODEL), lambda i: (i, 0))] + [pl.BlockSpec(w.shape, const) for w in weights],
        out_specs=[pl.BlockSpec((1, n, tm), feat3) if feat else pl.BlockSpec((1, tm, n), tok3)
                   for (n, _), feat in specs],
        out_shape=[jax.ShapeDtypeStruct((batch, n, seq) if feat else (batch, seq, n), dt)
                   for (n, dt), feat in specs],
        compiler_params=_params("parallel"),
        name="inproj_prompt",
    )(h, *weights)


def _matmul_kernel(x_ref, w_ref, o_ref):
    o_ref[...] = _dot(x_ref[...].astype(BF16), w_ref[...])


def matmul_rows(x, w, *, tm):
    m, k = x.shape
    n = w.shape[1]
    return pl.pallas_call(
        _matmul_kernel,
        grid=(m // tm,),
        in_specs=[pl.BlockSpec((tm, k), lambda i: (i, 0)), pl.BlockSpec((k, n), lambda i: (0, 0))],
        out_specs=pl.BlockSpec((tm, n), lambda i: (i, 0)),
        out_shape=jax.ShapeDtypeStruct((m, n), F32),
        compiler_params=_params("parallel"),
        name="matmul_rows",
    )(x, w)


def _cumsum_lanes_kernel(x_ref, o_ref, r_ref, carry_ref, *, tb):
    @pl.when(pl.program_id(1) == 0)
    def _():
        carry_ref[...] = jnp.zeros_like(carry_ref)

    h = x_ref.shape[1]
    r = lax.broadcasted_iota(jnp.int32, (tb, tb), 0)
    c = lax.broadcasted_iota(jnp.int32, (tb, tb), 1)
    upper = (r <= c).astype(F32)
    out = _dot_hi(x_ref[0], upper) + carry_ref[...]
    carry_ref[...] = out[:, tb - 1:tb]
    r_ref[0] = out * LOG2E
    cols = jnp.concatenate([out * LOG2E, jnp.zeros((LANES - h, tb), F32)], axis=0).T
    for i in range(h):
        o_ref[0, i] = cols[:, i:i + 1]


def cumsum_lanes(x, *, tb):
    b, h, t = x.shape
    return pl.pallas_call(
        functools.partial(_cumsum_lanes_kernel, tb=tb),
        grid=(b, t // tb),
        in_specs=[pl.BlockSpec((1, h, tb), lambda i, j: (i, 0, j))],
        out_specs=[pl.BlockSpec((1, h, tb, 1), lambda i, j: (i, 0, j, 0)),
                   pl.BlockSpec((1, h, tb), lambda i, j: (i, 0, j))],
        out_shape=[jax.ShapeDtypeStruct((b, h, t, 1), F32), jax.ShapeDtypeStruct((b, h, t), F32)],
        scratch_shapes=[pltpu.VMEM((h, 1), F32)],
        compiler_params=_params("parallel", "arbitrary"),
        name="cumsum_lanes",
    )(x)


def _fox_prompt_kernel(fast_ref, qt_ref, k_ref, vt_ref, c_ref, cr_ref, qn_ref, kn_ref, o_ref, km_ref, *, tq):
    i = pl.program_id(2)
    plan = fast_ref[(pl.program_id(0) * pl.num_programs(1) + pl.program_id(1)) * pl.num_programs(2) + i]
    fast = plan & 1
    first = plan >> 1

    @pl.when(i == 0)
    def _():
        kb = k_ref[0]
        low = lax.broadcasted_iota(jnp.int32, kb.shape, 1) < A_HEAD_DIM
        zero = jnp.zeros_like(kb)
        km_ref[0] = jnp.where(low, kb, zero)
        km_ref[1] = jnp.where(low, zero, kb)

    qt = qt_ref[0]
    key = lax.broadcasted_iota(jnp.int32, (tq, tq), 0)
    qry = lax.broadcasted_iota(jnp.int32, (tq, tq), 1)
    causal = key <= qry

    def scores_of(j, masked):
        off = pl.multiple_of(j * tq, tq)
        out = []
        for hh in range(2):
            s = _dot(km_ref[hh, pl.ds(off, tq), :], qt) - c_ref[0, hh, pl.ds(off, tq), :]
            out.append(jnp.where(causal, s, NEG_BIG) if masked else s)
        return out, [vt_ref[0, hh * A_HEAD_DIM:(hh + 1) * A_HEAD_DIM, pl.ds(off, tq)] for hh in range(2)]

    def finish(a0, l0, a1, l1):
        o_ref[0] = jnp.concatenate([a0 / l0, a1 / l1], axis=0).T.astype(o_ref.dtype)

    @pl.when(fast != 0)
    def _():
        ref = [qn_ref[0, 0, hh:hh + 1, :] * kn_ref[0, 0, hh:hh + 1, 0:1] - cr_ref[0, 0, hh:hh + 1, :]
               for hh in range(2)]

        def step(j, carry, masked):
            scores, vt = scores_of(j, masked)
            new = []
            for hh in range(2):
                l, acc = carry[hh]
                p = jnp.exp2(scores[hh] - ref[hh])
                new.append((l + jnp.sum(p, axis=0, keepdims=True), acc + _dot(vt[hh], p.astype(BF16))))
            return tuple(new)

        one = (jnp.zeros((1, tq), F32), jnp.zeros((A_HEAD_DIM, tq), F32))
        carry = lax.fori_loop(first, i, lambda j, c: step(j, c, False), (one, one))
        (l0, a0), (l1, a1) = step(i, carry, True)
        finish(a0, l0, a1, l1)

    @pl.when(fast == 0)
    def _():
        def step(j, carry, masked):
            scores, vt = scores_of(j, masked)
            new = []
            for hh in range(2):
                m, l, acc = carry[hh]
                m_new = jnp.maximum(m, jnp.max(scores[hh], axis=0, keepdims=True))
                alpha = jnp.exp2(m - m_new)
                p = jnp.exp2(scores[hh] - m_new)
                new.append((m_new, alpha * l + jnp.sum(p, axis=0, keepdims=True),
                            alpha * acc + _dot(vt[hh], p.astype(BF16))))
            return tuple(new)

        one = (jnp.full((1, tq), NEG_BIG, F32), jnp.zeros((1, tq), F32), jnp.zeros((A_HEAD_DIM, tq), F32))
        carry = lax.fori_loop(0, i, lambda j, c: step(j, c, False), (one, one))
        (_, l0, a0), (_, l1, a1) = step(i, carry, True)
        finish(a0, l0, a1, l1)


FOX_FAST_BOUND = 50.0
FOX_ZERO_EXP2 = -152.0


def fox_prompt(qt, k, vt, c_col, c_row, qn2, kn2, *, tq):
    b, t, _ = k.shape
    pairs = A_HEADS // 2
    nq = t // tq
    qn = jnp.sqrt(qn2)
    kn = jnp.sqrt(jnp.max(kn2, axis=-1)) * 1.01 + 1e-6
    fast = jnp.max((qn * kn[..., None]).reshape(b, pairs, 2, nq, tq), axis=(2, 4)) <= FOX_FAST_BOUND
    c_blocks = c_row.reshape(b, pairs, 2, nq, tq)
    gap = c_blocks[..., None, :, tq - 1] - c_blocks[..., :, None, 0]
    dead = jnp.all(gap >= -FOX_ZERO_EXP2, axis=2) & (jnp.arange(nq)[None, :] < jnp.arange(nq)[:, None])
    first = jnp.sum(jnp.cumprod(dead.astype(jnp.int32), axis=-1), axis=-1)
    plan = fast.astype(jnp.int32) + 2 * jnp.where(fast, first, 0)
    kn_l = jnp.broadcast_to(kn.reshape(b, pairs, 2, 1), (b, pairs, 2, LANES))
    pair_row = lambda bi, hp, i, f: (bi, hp, 0, i)
    grid_spec = pltpu.PrefetchScalarGridSpec(
        num_scalar_prefetch=1,
        grid=(b, pairs, nq),
        in_specs=[
            pl.BlockSpec((1, LANES, tq), lambda bi, hp, i, f: (bi, hp, i)),
            pl.BlockSpec((1, t, LANES), lambda bi, hp, i, f: (bi, 0, hp)),
            pl.BlockSpec((1, LANES, t), lambda bi, hp, i, f: (bi, hp, 0)),
            pl.BlockSpec((1, 2, t, 1), lambda bi, hp, i, f: (bi, hp, 0, 0)),
            pl.BlockSpec((1, 1, 2, tq), pair_row),
            pl.BlockSpec((1, 1, 2, tq), pair_row),
            pl.BlockSpec((1, 1, 2, LANES), lambda bi, hp, i, f: (bi, hp, 0, 0)),
        ],
        out_specs=pl.BlockSpec((1, tq, LANES), lambda bi, hp, i, f: (bi, i, hp)),
        scratch_shapes=[pltpu.VMEM((2, t, LANES), BF16)],
    )
    return pl.pallas_call(
        functools.partial(_fox_prompt_kernel, tq=tq),
        grid_spec=grid_spec,
        out_shape=jax.ShapeDtypeStruct((b, t, A_WIDTH), BF16),
        compiler_params=_params("parallel", "parallel", "arbitrary"),
        name="fox_prompt",
    )(plan.reshape(-1), qt, k, vt, c_col, c_row.reshape(b, pairs, 2, t),
      qn.reshape(b, pairs, 2, t), kn_l)


def _head_mask():
    hrow = lax.broadcasted_iota(jnp.int32, (A_HEADS, A_WIDTH), 0)
    hcol = lax.broadcasted_iota(jnp.int32, (A_HEADS, A_WIDTH), 1) // A_HEAD_DIM
    return hrow == hcol


def _block_diag_queries(q_ref, n_q):
    mask = _head_mask()
    return jnp.concatenate(
        [jnp.where(mask, jnp.broadcast_to(q_ref[0, qi:qi + 1, :], (A_HEADS, A_WIDTH)), 0.0) for qi in range(n_q)],
        axis=0)


def _fox_scores_kernel(pt_ref, q_ref, *rest, n_q, group):
    kp_refs, lfp_refs = rest[:group], rest[group:2 * group]
    s_ref, qbd_ref, carry_ref = rest[2 * group:]

    @pl.when(pl.program_id(1) == 0)
    def _():
        carry_ref[...] = jnp.zeros_like(carry_ref)
        qbd_ref[...] = _block_diag_queries(q_ref, n_q).astype(BF16)

    qbd = qbd_ref[...]
    r = lax.broadcasted_iota(jnp.int32, (PAGE_SIZE, PAGE_SIZE), 0)
    c = lax.broadcasted_iota(jnp.int32, (PAGE_SIZE, PAGE_SIZE), 1)
    after = (r > c).astype(F32)
    carry = carry_ref[...]
    for g in range(group):
        lf = lfp_refs[g][...]
        bias = _dot_hi(lf, after) + carry
        carry = carry + jnp.sum(lf, axis=-1, keepdims=True)
        s = _dot(qbd, kp_refs[g][...].astype(BF16)) + jnp.concatenate([bias] * n_q, axis=0)
        s_ref[0, :, (group - 1 - g) * PAGE_SIZE:(group - g) * PAGE_SIZE] = s
    carry_ref[...] = carry


def _fox_values_kernel(vpage_ref, pos_ref, nlive_ref, q_ref, kn_ref, vn_ref, lfn_ref, s_ref, mpast_ref, *rest,
                       n_q, group, n_pages):
    vp_refs = rest[:group]
    o_ref, m_ref, l_ref, acc_ref = rest[group:]
    b = pl.program_id(0)
    p = pl.program_id(1)
    rows = n_q * A_HEADS

    @pl.when(p == 0)
    def _():
        qbd = _block_diag_queries(q_ref, n_q)
        lfn = lfn_ref[0]
        qidx = lax.broadcasted_iota(jnp.int32, (rows, 1), 0) // A_HEADS
        s_new = []
        cum = jnp.zeros((A_HEADS, 1), F32)
        for ki in range(n_q):
            cum = cum + lfn[:, ki:ki + 1]
            sk = jnp.sum(qbd * kn_ref[0, ki:ki + 1, :], axis=-1, keepdims=True)
            sk = sk - jnp.concatenate([cum] * n_q, axis=0)
            s_new.append(jnp.where(qidx >= ki, sk, NEG_BIG))
        m = mpast_ref[0]
        for sk in s_new:
            m = jnp.maximum(m, sk)
        l = jnp.zeros((rows, 1), F32)
        acc = jnp.zeros((rows, A_WIDTH), F32)
        for ki, sk in enumerate(s_new):
            pk = jnp.exp(sk - m)
            l = l + pk
            acc = acc + pk * vn_ref[0, ki:ki + 1, :]
        m_ref[...] = m
        l_ref[...] = l
        acc_ref[...] = acc

    @pl.when(p * group < nlive_ref[b])
    def _():
        m = m_ref[...]
        l = l_ref[...]
        pv = None
        for g in range(group):
            pos = pos_ref[b * n_pages + p * group + g]
            s = s_ref[0, :, pl.ds(pl.multiple_of(pos * PAGE_SIZE, PAGE_SIZE), PAGE_SIZE)]
            pr = jnp.exp(s - m)
            l = l + jnp.sum(pr, axis=-1, keepdims=True)
            t = _dot_nt(pr.astype(BF16), vp_refs[g][...].astype(BF16))
            pv = t if pv is None else pv + t
        l_ref[...] = l
        acc_ref[...] += pv

    @pl.when(p == pl.num_programs(1) - 1)
    def _():
        mask = _head_mask()
        acc = acc_ref[...] / l_ref[...]
        for qi in range(n_q):
            blk = jnp.where(mask, acc[qi * A_HEADS:(qi + 1) * A_HEADS, :], 0.0)
            o_ref[0, qi:qi + 1, :] = jnp.sum(blk, axis=0, keepdims=True)


FOX_ZERO_EXP = -106.0


def fox_sample_two_pass(page_table, q, k_new, v_new, lf_new_t, cache_kt, cache_vt, cache_lft, *, layer, group):
    bd, n_q, _ = q.shape
    n_pages = page_table.shape[1]
    rows = n_q * A_HEADS
    n_groups = n_pages // group
    assert n_pages % group == 0
    seq3 = lambda b, p, *_: (b, 0, 0)

    def page(g):
        return lambda b, p, pt_ref: (layer, pt_ref[b * n_pages + (n_pages - 1 - (p * group + g))], 0, 0)

    scores = pl.pallas_call(
        functools.partial(_fox_scores_kernel, n_q=n_q, group=group),
        grid_spec=pltpu.PrefetchScalarGridSpec(
            num_scalar_prefetch=1,
            grid=(bd, n_groups),
            in_specs=[pl.BlockSpec((1, n_q, A_WIDTH), seq3)]
            + [pl.BlockSpec((None, None, A_WIDTH, PAGE_SIZE), page(g)) for g in range(group)]
            + [pl.BlockSpec((None, None, A_HEADS, PAGE_SIZE), page(g)) for g in range(group)],
            out_specs=pl.BlockSpec((1, rows, group * PAGE_SIZE), lambda b, p, pt_ref: (b, 0, n_groups - 1 - p)),
            scratch_shapes=[pltpu.VMEM((rows, A_WIDTH), BF16), pltpu.VMEM((A_HEADS, 1), F32)],
        ),
        out_shape=jax.ShapeDtypeStruct((bd, rows, n_pages * PAGE_SIZE), F32),
        compiler_params=_params("parallel", "arbitrary"),
        name="fox_sample_scores",
    )(page_table.reshape(-1), q, *([cache_kt] * group + [cache_lft] * group))

    page_max = jnp.max(scores.reshape(bd, rows, n_pages, PAGE_SIZE), axis=-1)
    m_past = jnp.max(page_max, axis=-1, keepdims=True)
    live = jnp.any(page_max - m_past > FOX_ZERO_EXP, axis=1)
    n_live = jnp.sum(live, axis=-1).astype(jnp.int32)
    newest_first = jnp.arange(n_pages - 1, -1, -1, dtype=jnp.int32)
    rank = jnp.where(live, 0, n_pages) + newest_first[None, :]
    pos = jnp.argsort(rank, axis=-1).astype(jnp.int32)
    slot = jnp.minimum(jnp.arange(n_pages, dtype=jnp.int32)[None, :], n_live[:, None] - 1)
    pos = jnp.take_along_axis(pos, slot, axis=-1)
    vpage = jnp.take_along_axis(page_table, pos, axis=-1)

    def vpage_map(g):
        return lambda b, p, vp_ref, pos_ref, nl_ref: (layer, vp_ref[b * n_pages + p * group + g], 0, 0)

    return pl.pallas_call(
        functools.partial(_fox_values_kernel, n_q=n_q, group=group, n_pages=n_pages),
        grid_spec=pltpu.PrefetchScalarGridSpec(
            num_scalar_prefetch=3,
            grid=(bd, n_groups),
            in_specs=[
                pl.BlockSpec((1, n_q, A_WIDTH), seq3),
                pl.BlockSpec((1, n_q, A_WIDTH), seq3),
                pl.BlockSpec((1, n_q, A_WIDTH), seq3),
                pl.BlockSpec((1, A_HEADS, n_q), seq3),
                pl.BlockSpec((1, rows, n_pages * PAGE_SIZE), seq3),
                pl.BlockSpec((1, rows, 1), seq3),
            ] + [pl.BlockSpec((None, None, A_WIDTH, PAGE_SIZE), vpage_map(g)) for g in range(group)],
            out_specs=pl.BlockSpec((1, n_q, A_WIDTH), seq3),
            scratch_shapes=[pltpu.VMEM((rows, 1), F32), pltpu.VMEM((rows, 1), F32), pltpu.VMEM((rows, A_WIDTH), F32)],
        ),
        out_shape=jax.ShapeDtypeStruct((bd, n_q, A_WIDTH), F32),
        compiler_params=_params("parallel", "arbitrary"),
        name="fox_sample_values",
    )(vpage.reshape(-1), pos.reshape(-1), n_live, q, k_new, v_new, lf_new_t, scores, m_past,
      *([cache_vt] * group))


def _pool_kernel(u_ref, pre_ref, w_ref, sc_ref, o_ref, ext_ref, *, tt, n_valid):
    halo = POOL_BUF + 1
    j = pl.program_id(1)

    @pl.when(j == 0)
    def _():
        ext_ref[0:halo, :] = pre_ref[0]

    @pl.when(j > 0)
    def _():
        ext_ref[0:halo, :] = ext_ref[tt:tt + halo, :]

    u = u_ref[0]
    ext_ref[halo:halo + tt, :] = u
    pos = j * tt + lax.broadcasted_iota(jnp.int32, (tt, 1), 0) + (1 + n_valid)
    outs = []
    for gi, w in enumerate(POOL_WINDOWS):
        sl = slice(gi * POOL_GROUP_DIM, (gi + 1) * POOL_GROUP_DIM)
        tot = u[:, sl]
        for k in range(1, w):
            tot = tot + ext_ref[halo - k:halo - k + tt, sl]
        cnt = jnp.minimum(pos, w).astype(F32)
        d = tot / cnt - u[:, sl]
        outs.append(_dot(d.astype(BF16), w_ref[gi]))
    o_ref[0] = (jnp.concatenate(outs, axis=-1) * sc_ref[...]).astype(o_ref.dtype)


def pool_mix(u, prefix16, pool_w, pool_scale, *, tt, n_valid, out_dtype):
    b, t, _ = u.shape
    halo = POOL_BUF + 1
    return pl.pallas_call(
        functools.partial(_pool_kernel, tt=tt, n_valid=n_valid),
        grid=(b, t // tt),
        in_specs=[
            pl.BlockSpec((1, tt, B_WIDTH), lambda i, j: (i, j, 0)),
            pl.BlockSpec((1, halo, B_WIDTH), lambda i, j: (i, 0, 0)),
            pl.BlockSpec((len(POOL_WINDOWS), POOL_GROUP_DIM, POOL_GROUP_DIM), lambda i, j: (0, 0, 0)),
            pl.BlockSpec((1, B_WIDTH), lambda i, j: (0, 0)),
        ],
        out_specs=pl.BlockSpec((1, tt, B_WIDTH), lambda i, j: (i, j, 0)),
        out_shape=jax.ShapeDtypeStruct((b, t, B_WIDTH), out_dtype),
        scratch_shapes=[pltpu.VMEM((tt + 2 * halo, B_WIDTH), F32)],
        compiler_params=_params("parallel", "arbitrary"),
        name="pool_mix",
    )(u, prefix16, pool_w, pool_scale)


def _gdn_kernel(x_ref, ab_ref, z_ref, pre_ref, s0_ref, cw_ref, alog_ref, dtb_ref, nw_ref,
                o_ref, s_out_ref, ext_ref, s_ref, *, chunk, n_chunks, t_valid):
    j = pl.program_id(1)
    halo = 8

    @pl.when(j == 0)
    def _():
        ext_ref[0:halo, :] = pre_ref[0]
        s_ref[...] = s0_ref[0]

    span = chunk * n_chunks

    @pl.when(j > 0)
    def _():
        ext_ref[0:halo, :] = ext_ref[span:span + halo, :]

    ext_ref[halo:halo + span, :] = x_ref[0]
    y_all = ext_ref[halo:halo + span, :] * cw_ref[CONV_WIDTH - 1:CONV_WIDTH, :]
    for k in range(1, CONV_WIDTH):
        y_all = y_all + ext_ref[halo - k:halo - k + span, :] * cw_ref[CONV_WIDTH - 1 - k:CONV_WIDTH - k, :]
    y_all = _silu(y_all)

    ab = ab_ref[0]
    valid = (j * span + lax.broadcasted_iota(jnp.int32, (span, 1), 0)) < t_valid
    g_all = jnp.where(valid, -jnp.exp(alog_ref[...]) * _softplus(ab + dtb_ref[...]), 0.0)
    beta_all = jnp.where(valid, _sigmoid(ab), 0.0)
    z_all = z_ref[0]
    r = lax.broadcasted_iota(jnp.int32, (chunk, chunk), 0)
    c = lax.broadcasted_iota(jnp.int32, (chunk, chunk), 1)
    tri = (r >= c).astype(F32)

    n = C_HEADS * chunk
    rows = [slice(h * chunk, (h + 1) * chunk) for h in range(C_HEADS)]
    row_head = lax.broadcasted_iota(jnp.int32, (n, LANES), 0) // chunk
    lane = lax.broadcasted_iota(jnp.int32, (n, LANES), 1)
    pick = lambda a, off: jnp.where(lane == row_head + off, jnp.concatenate([a] * C_HEADS, axis=0), 0.0)
    rr = lax.broadcasted_iota(jnp.int32, (n, n), 0)
    cc = lax.broadcasted_iota(jnp.int32, (n, n), 1)
    same = (rr // chunk) == (cc // chunk)

    def setup(ci):
        tok = slice(ci * chunk, (ci + 1) * chunk)
        y = y_all[tok]
        gcum = _dot_hi(tri, g_all[tok])
        heads = lambda off: jnp.concatenate(
            [y[:, off + h * C_HEAD_DIM:off + (h + 1) * C_HEAD_DIM] for h in range(C_HEADS)], axis=0)
        qs, ks, vs = heads(0), heads(C_WIDTH), heads(2 * C_WIDTH)
        qs = qs * lax.rsqrt(jnp.sum(qs * qs, axis=-1, keepdims=True) + L2_EPS) * (C_HEAD_DIM ** -0.5)
        ks = ks * lax.rsqrt(jnp.sum(ks * ks, axis=-1, keepdims=True) + L2_EPS)
        gd = pick(gcum, 0)
        gc = jnp.sum(gd, axis=-1, keepdims=True)
        gr = _dot_nt(jnp.ones((8, LANES), F32), gd, precision=HIGHEST)[0:1, :]
        beta = jnp.sum(pick(beta_all[tok], C_HEADS), axis=-1, keepdims=True)
        g_last = jnp.sum(pick(jnp.broadcast_to(gcum[chunk - 1:chunk, :], (chunk, LANES)), 0),
                         axis=-1, keepdims=True)
        gamma = jnp.exp(jnp.where(same & (rr >= cc), gc - gr, -jnp.inf))
        eg = jnp.exp(gc)
        ksb = ks.astype(BF16)
        return dict(
            nmat=jnp.where(same & (rr > cc), beta * _dot_nt(ksb, ksb) * gamma, 0.0),
            bv=_split(beta * vs), bk=_split(beta * ks * eg),
            qk=(_dot_nt(qs.astype(BF16), ksb) * gamma).astype(BF16),
            k_dec=ks * jnp.exp(g_last - gc), q_dec=qs * eg,
            decay=[jnp.exp(gcum[chunk - 1:chunk, h:h + 1]) for h in range(C_HEADS)],
            z=jnp.concatenate([z_all[tok, h * C_HEAD_DIM:(h + 1) * C_HEAD_DIM] for h in range(C_HEADS)], axis=0))

    parts = [setup(ci) for ci in range(n_chunks)]
    inv = [(rr == cc).astype(F32) - p["nmat"] for p in parts]
    pw = [_split(p["nmat"]) for p in parts]
    for _ in range(max(1, (chunk - 1).bit_length()) - 1):
        pw = [_split(_dot(x[0], x[0])) for x in pw]
        inv = [a + _mm_split(_split(a), x) for a, x in zip(inv, pw)]
    inv = [_split(a) for a in inv]
    for p, a in zip(parts, inv):
        p["u"] = _mm_split(a, p["bv"])
        p["w"] = _mm_split(a, p["bk"])
    state = [s_ref[h] for h in range(C_HEADS)]
    for ci, p in enumerate(parts):
        state_b = [s.astype(BF16) for s in state]
        v_new = jnp.concatenate(
            [p["u"][rows[h]] - _dot(p["w"][rows[h]].astype(BF16), state_b[h]) for h in range(C_HEADS)], axis=0)
        o = jnp.concatenate([_dot(p["q_dec"][rows[h]].astype(BF16), state_b[h]) for h in range(C_HEADS)], axis=0)
        o = o + _dot(p["qk"], v_new.astype(BF16))
        state = [state[h] * p["decay"][h] + lax.dot_general(
            p["k_dec"][rows[h]].astype(BF16), v_new[rows[h]].astype(BF16), (((0,), (0,)), ((), ())),
            preferred_element_type=F32) for h in range(C_HEADS)]
        o = o * lax.rsqrt(jnp.mean(o * o, axis=-1, keepdims=True) + RMS_EPS) * nw_ref[...]
        o = o * _silu(p["z"])
        o_ref[0, ci * chunk:(ci + 1) * chunk, :] = jnp.concatenate(
            [o[rows[h]] for h in range(C_HEADS)], axis=-1).astype(o_ref.dtype)
    for h in range(C_HEADS):
        s_ref[h] = state[h]

    @pl.when(j == pl.num_programs(1) - 1)
    def _():
        s_out_ref[0] = s_ref[...]


def gdn_mixer(x, ab, z, prefix8, s0, conv_w, a_log_row, dt_bias_row, norm_w, *, chunk, n_chunks, t_valid,
              out_dtype):
    b, t, _ = x.shape
    span = chunk * n_chunks
    row = lambda i, j: (0, 0)
    tok = lambda i, j: (i, j, 0)
    seq = lambda i, j: (i, 0, 0)
    return pl.pallas_call(
        functools.partial(_gdn_kernel, chunk=chunk, n_chunks=n_chunks, t_valid=t_valid),
        grid=(b, t // span),
        in_specs=[
            pl.BlockSpec((1, span, 3 * C_WIDTH), tok),
            pl.BlockSpec((1, span, LANES), tok),
            pl.BlockSpec((1, span, C_WIDTH), tok),
            pl.BlockSpec((1, 8, 3 * C_WIDTH), seq),
            pl.BlockSpec((1, C_HEADS, C_HEAD_DIM, C_HEAD_DIM), lambda i, j: (i, 0, 0, 0)),
            pl.BlockSpec((CONV_WIDTH, 3 * C_WIDTH), row),
            pl.BlockSpec((1, LANES), row),
            pl.BlockSpec((1, LANES), row),
            pl.BlockSpec((1, C_HEAD_DIM), row),
        ],
        out_specs=[
            pl.BlockSpec((1, span, C_WIDTH), tok),
            pl.BlockSpec((1, C_HEADS, C_HEAD_DIM, C_HEAD_DIM), lambda i, j: (i, 0, 0, 0)),
        ],
        out_shape=[
            jax.ShapeDtypeStruct((b, t, C_WIDTH), out_dtype),
            jax.ShapeDtypeStruct((b, C_HEADS, C_HEAD_DIM, C_HEAD_DIM), F32),
        ],
        scratch_shapes=[
            pltpu.VMEM((span + 16, 3 * C_WIDTH), F32),
            pltpu.VMEM((C_HEADS, C_HEAD_DIM, C_HEAD_DIM), F32),
        ],
        compiler_params=_params("parallel", "arbitrary"),
        name="gdn_mixer",
    )(x, ab, z, prefix8, s0, conv_w, a_log_row, dt_bias_row, norm_w)


def _merge_kernel(h_ref, oa_ref, ob_ref, oc_ref, wg_ref, wa_ref, wb_ref, wc_ref, wo_ref, g_ref, b_ref, o_ref):
    h = h_ref[...]
    hb = h.astype(BF16)
    merged = None
    for i, (x_ref, w_ref) in enumerate(((oa_ref, wa_ref), (ob_ref, wb_ref), (oc_ref, wc_ref))):
        gate = _sigmoid(_dot(hb, wg_ref[:, i * D_MODEL:(i + 1) * D_MODEL]))
        term = gate * _dot(x_ref[...].astype(BF16), w_ref[...])
        merged = term if merged is None else merged + term
    y = DN_ALPHA * h + _dot(merged.astype(BF16), wo_ref[...])
    o_ref[...] = _layer_norm(y, g_ref[...], b_ref[...])


def merge_ln(h, oa, ob, oc, wg, wa, wb, wc, wo, g, b, *, tm):
    m = h.shape[0]
    const = lambda i: (0, 0)
    rows = lambda i: (i, 0)
    return pl.pallas_call(
        _merge_kernel,
        grid=(m // tm,),
        in_specs=[
            pl.BlockSpec((tm, D_MODEL), rows),
            pl.BlockSpec((tm, A_WIDTH), rows),
            pl.BlockSpec((tm, B_WIDTH), rows),
            pl.BlockSpec((tm, C_WIDTH), rows),
            pl.BlockSpec((D_MODEL, N_BRANCH * D_MODEL), const),
            pl.BlockSpec((A_WIDTH, D_MODEL), const),
            pl.BlockSpec((B_WIDTH, D_MODEL), const),
            pl.BlockSpec((C_WIDTH, D_MODEL), const),
            pl.BlockSpec((D_MODEL, D_MODEL), const),
            pl.BlockSpec((1, D_MODEL), const),
            pl.BlockSpec((1, D_MODEL), const),
        ],
        out_specs=pl.BlockSpec((tm, D_MODEL), rows),
        out_shape=jax.ShapeDtypeStruct((m, D_MODEL), F32),
        compiler_params=_params("parallel"),
        name="merge_ln",
    )(h, oa, ob, oc, wg, wa, wb, wc, wo, g, b)


def _split_w_in(w_in_l):
    o = 0
    parts = {}
    for name, n in (("q", A_WIDTH), ("k", A_WIDTH), ("v", A_WIDTH), ("f", A_HEADS), ("ub", B_WIDTH),
                    ("qkvc", 3 * C_WIDTH), ("a", C_HEADS), ("b", C_HEADS), ("z", C_WIDTH),
                    ("gate", N_BRANCH * D_MODEL)):
        parts[name] = w_in_l[:, o:o + n]
        o += n
    return parts


def kernel(x_prompt, x_sample, cache_k, cache_v, cache_logf, page_table, state_pool, state_conv, state_ssm,
           w_in, fox_f_bias, gdn_conv_w, gdn_a_log, gdn_dt_bias, gdn_norm_w, pool_w, pool_scale,
           w_branch_a, w_branch_b, w_branch_c, w_out, ffn_w_in, ffn_w_out, ln_g, ln_b):
    bp, seq, _ = x_prompt.shape
    bd, n_q, _ = x_sample.shape
    mp, ms = bp * seq, bd * n_q
    tm_p = 512
    tm_s = ms

    n_pool = cache_k.shape[1]
    cache_kt = cache_k.transpose(0, 1, 3, 4, 2).reshape(DEPTH, n_pool, A_WIDTH, PAGE_SIZE)
    cache_vt = cache_v.transpose(0, 1, 3, 4, 2).reshape(DEPTH, n_pool, A_WIDTH, PAGE_SIZE)
    cache_lft = cache_logf.transpose(0, 1, 3, 2)

    yp = x_prompt.reshape(mp, D_MODEL)
    ys = x_sample.reshape(ms, D_MODEL)
    outs_p = ([], [], [], [], [], [])
    outs_s = ([], [], [], [], [], [])
    zeros_ab = jnp.zeros((D_MODEL, LANES - 2 * C_HEADS), F32)
    for l in range(DEPTH):
        wp = _split_w_in(w_in[l])
        wq = (wp["q"] * (A_HEAD_DIM ** -0.5)).astype(BF16)
        w_ab = jnp.concatenate([wp["a"], wp["b"], zeros_ab], axis=1)
        w_rest = jnp.concatenate([wp["qkvc"], wp["ub"], wp["z"], w_ab], axis=1).astype(BF16)
        o_qkvc, o_ub, o_z, o_ab = 0, 3 * C_WIDTH, 3 * C_WIDTH + B_WIDTH, 3 * C_WIDTH + B_WIDTH + C_WIDTH
        w_f_pad = jnp.concatenate([wp["f"], jnp.zeros((D_MODEL, LANES - A_HEADS), F32)], axis=1)
        w_sample = jnp.concatenate([wq.astype(F32), wp["k"], wp["v"], w_f_pad], axis=1).astype(BF16)
        w_gate = wp["gate"].astype(BF16)
        ffn_up = ffn_w_in[l].astype(BF16)
        ffn_dn = ffn_w_out[l].astype(BF16)
        lg = ln_g[l].reshape(3, 1, D_MODEL)
        lb = ln_b[l].reshape(3, 1, D_MODEL)
        fb = fox_f_bias[l]
        a_log_row = jnp.zeros((1, LANES), F32).at[0, :C_HEADS].set(gdn_a_log[l])
        dt_row = jnp.zeros((1, LANES), F32).at[0, :C_HEADS].set(gdn_dt_bias[l])
        norm_w = gdn_norm_w[l].reshape(1, C_HEAD_DIM)
        pw = pool_w[l].astype(BF16)
        psc = pool_scale[l].reshape(1, B_WIDTH)
        w_merge = (w_gate, w_branch_a[l].astype(BF16), w_branch_b[l].astype(BF16),
                   w_branch_c[l].astype(BF16), w_out[l].astype(BF16))

        hp = ffn_ln(yp, ffn_up[0], ffn_dn[0], lg[0], lb[0], tm=tm_p)
        hs = ffn_ln(ys, ffn_up[0], ffn_dn[0], lg[0], lb[0], tm=tm_s)

        wqt = (wp["q"] * (A_HEAD_DIM ** -0.5 * LOG2E)).T.astype(BF16)
        qt_p, kb_p, kt_p, vt_p, vtb_p, lft_p, qn2_p, kn2_p, qkvc_p, ub_p, z_p, ab_p = inproj_prompt(
            hp, wqt, wp["k"].astype(BF16), wp["k"].T.astype(BF16), wp["v"].T.astype(BF16), wp["f"].T.astype(BF16),
            fb.reshape(A_HEADS, 1), wp["qkvc"].astype(BF16), wp["ub"].astype(BF16), wp["z"].astype(BF16),
            w_ab.astype(BF16), batch=bp, seq=seq, tm=256)
        ccol_p, crow_p = cumsum_lanes(lft_p, tb=512)
        oa_p = fox_prompt(qt_p, kb_p, vtb_p, ccol_p, crow_p, qn2_p, kn2_p, tq=1024).reshape(mp, A_WIDTH)
        ob_p = pool_mix(ub_p, jnp.zeros((bp, POOL_BUF + 1, B_WIDTH), F32), pw, psc,
                        tt=512, n_valid=0, out_dtype=BF16).reshape(mp, B_WIDTH)
        oc_p, ssm_p = gdn_mixer(qkvc_p, ab_p, z_p,
                                jnp.zeros((bp, 8, 3 * C_WIDTH), F32),
                                jnp.zeros((bp, C_HEADS, C_HEAD_DIM, C_HEAD_DIM), F32),
                                gdn_conv_w[l], a_log_row, dt_row, norm_w, chunk=64, n_chunks=4, t_valid=seq,
                                out_dtype=BF16)
        yp = merge_ln(hp, oa_p, ob_p, oc_p.reshape(mp, C_WIDTH), *w_merge, lg[1], lb[1], tm=tm_p)
        yp = ffn_ln(yp, ffn_up[1], ffn_dn[1], lg[2], lb[2], tm=tm_p)
        k_p = kt_p.reshape(bp, A_HEADS, A_HEAD_DIM, seq).transpose(0, 3, 1, 2)
        v_p = vt_p.reshape(bp, A_HEADS, A_HEAD_DIM, seq).transpose(0, 3, 1, 2)
        for lst, a in zip(outs_p, (k_p, v_p, lft_p.transpose(0, 2, 1), ub_p[:, seq - POOL_BUF:],
                                   qkvc_p[:, seq - CONV_BUF:], ssm_p)):
            lst.append(a)

        proj_s = matmul_rows(hs, w_sample, tm=tm_s)
        rest_s = matmul_rows(hs, w_rest, tm=tm_s).reshape(bd, n_q, -1)
        q_s = proj_s[:, :A_WIDTH].reshape(bd, n_q, A_WIDTH)
        k_s = proj_s[:, A_WIDTH:2 * A_WIDTH].reshape(bd, n_q, A_WIDTH)
        v_s = proj_s[:, 2 * A_WIDTH:3 * A_WIDTH].reshape(bd, n_q, A_WIDTH)
        lf_s = jax.nn.log_sigmoid(proj_s[:, 3 * A_WIDTH:3 * A_WIDTH + A_HEADS] + fb).reshape(bd, n_q, A_HEADS)
        oa_s = fox_sample_two_pass(page_table, q_s, k_s, v_s, lf_s.transpose(0, 2, 1), cache_kt, cache_vt,
                                   cache_lft, layer=l, group=8).reshape(ms, A_WIDTH)
        ub_s = rest_s[:, :, o_ub:o_ub + B_WIDTH]
        qkvc_s = rest_s[:, :, o_qkvc:o_qkvc + 3 * C_WIDTH]
        pool_full = jnp.concatenate([state_pool[l], ub_s], axis=1)
        conv_full = jnp.concatenate([state_conv[l], qkvc_s], axis=1)
        pre16 = jnp.concatenate([jnp.zeros((bd, 1, B_WIDTH), F32), state_pool[l]], axis=1)
        ob_s = pool_mix(ub_s, pre16, pw, psc, tt=n_q, n_valid=POOL_BUF, out_dtype=F32).reshape(ms, B_WIDTH)
        pad_t = lambda a: jnp.pad(a, ((0, 0), (0, 8 - n_q), (0, 0)))
        pre8 = jnp.concatenate([jnp.zeros((bd, 8 - CONV_BUF, 3 * C_WIDTH), F32), state_conv[l]], axis=1)
        oc_s, ssm_s = gdn_mixer(pad_t(qkvc_s), pad_t(rest_s[:, :, o_ab:o_ab + LANES]),
                                pad_t(rest_s[:, :, o_z:o_z + C_WIDTH]), pre8, state_ssm[l],
                                gdn_conv_w[l], a_log_row, dt_row, norm_w, chunk=8, n_chunks=1, t_valid=n_q,
                                out_dtype=F32)
        ys = merge_ln(hs, oa_s, ob_s, oc_s[:, :n_q].reshape(ms, C_WIDTH), *w_merge, lg[1], lb[1], tm=tm_s)
        ys = ffn_ln(ys, ffn_up[1], ffn_dn[1], lg[2], lb[2], tm=tm_s)
        for lst, a in zip(outs_s, (k_s.reshape(bd, n_q, A_HEADS, A_HEAD_DIM), v_s.reshape(bd, n_q, A_HEADS, A_HEAD_DIM),
                                   lf_s, pool_full[:, -POOL_BUF:], conv_full[:, -CONV_BUF:], ssm_s)):
            lst.append(a)

    k_p, v_p, lf_p, pool_p, conv_p, ssm_p = [jnp.stack(a) for a in outs_p]
    k_s, v_s, lf_s, pool_s, conv_s, ssm_s = [jnp.stack(a) for a in outs_s]
    return (yp.reshape(bp, seq, D_MODEL), ys.reshape(bd, n_q, D_MODEL), k_p, v_p, lf_p, pool_p, conv_p, ssm_p,
            k_s, v_s, lf_s, pool_s, conv_s, ssm_s)
```

```python
import functools

import jax
import jax.numpy as jnp
from jax import lax
from jax.experimental import pallas as pl
from jax.experimental.pallas import tpu as pltpu

F32 = jnp.float32
BF16 = jnp.bfloat16
HIGHEST = lax.Precision.HIGHEST

D_MODEL = 1024
DEPTH = 2
PAGE_SIZE = 128
A_HEADS = 16
A_HEAD_DIM = 64
A_WIDTH = A_HEADS * A_HEAD_DIM
POOL_WINDOWS = (2, 4, 8, 16)
POOL_GROUP_DIM = 128
B_WIDTH = len(POOL_WINDOWS) * POOL_GROUP_DIM
POOL_BUF = max(POOL_WINDOWS) - 1
C_HEADS = 4
C_HEAD_DIM = 128
C_WIDTH = C_HEADS * C_HEAD_DIM
CONV_WIDTH = 4
CONV_BUF = CONV_WIDTH - 1
N_BRANCH = 3
D_FF = ((8 * D_MODEL // 3 + 127) // 128) * 128
DN_ALPHA = (2 * DEPTH) ** 0.25
LN_EPS = 1e-5
RMS_EPS = 1e-6
L2_EPS = 1e-6
NEG_BIG = -1e30
LOG2E = 1.4426950408889634

VMEM_LIMIT_BYTES = 56 * 1024 * 1024
LANES = 128


def _params(*sem):
    return pltpu.CompilerParams(dimension_semantics=sem, vmem_limit_bytes=VMEM_LIMIT_BYTES)


def _layer_norm(y, g, b):
    mu = jnp.mean(y, axis=-1, keepdims=True)
    d = y - mu
    var = jnp.mean(d * d, axis=-1, keepdims=True)
    return d * lax.rsqrt(var + LN_EPS) * g + b


def _log_sigmoid(x):
    return jnp.minimum(x, 0.0) - jnp.log(1.0 + jnp.exp(-jnp.abs(x)))


def _softplus(x):
    return jnp.maximum(x, 0.0) + jnp.log(1.0 + jnp.exp(-jnp.abs(x)))


def _sigmoid(x):
    return 1.0 / (1.0 + jnp.exp(-x))


def _silu(x):
    return x * _sigmoid(x)


def _dot(a, b):
    return jnp.dot(a, b, preferred_element_type=F32)


def _dot_nt(a, b, precision=None):
    return lax.dot_general(a, b, (((1,), (1,)), ((), ())), preferred_element_type=F32, precision=precision)


def _dot_hi(a, b):
    return jnp.dot(a, b, preferred_element_type=F32, precision=HIGHEST)


def _split(a):
    hi = a.astype(BF16)
    return hi, (a - hi.astype(F32)).astype(BF16)


def _mm_split(a, b):
    (ah, al), (bh, bl) = a, b
    return _dot(ah, bh) + (_dot(ah, bl) + _dot(al, bh))


FFN_COLUMN_SPLITS = (0, 768, 1536, 2304, D_FF)


def _ffn_ln_kernel(x_ref, wup_ref, wd_ref, g_ref, b_ref, o_ref):
    x = x_ref[...]
    xb = x.astype(BF16)
    groups = list(zip(FFN_COLUMN_SPLITS[:-1], FFN_COLUMN_SPLITS[1:]))
    gate_up = lambda lo, hi: (_dot(xb, wup_ref[:, lo:hi]), _dot(xb, wup_ref[:, D_FF + lo:D_FF + hi]))
    pending = gate_up(*groups[0])
    acc = None
    for gi, (lo, hi) in enumerate(groups):
        gate, up = pending
        if gi + 1 < len(groups):
            pending = gate_up(*groups[gi + 1])
        part = _dot((_silu(gate) * up).astype(BF16), wd_ref[lo:hi, :])
        acc = part if acc is None else acc + part
    o_ref[...] = _layer_norm(DN_ALPHA * x + 0.5 * acc, g_ref[...], b_ref[...])


def ffn_ln(x, w_up, w_down, g, b, *, tm):
    m = x.shape[0]
    const = lambda i: (0, 0)
    resident = dict(pipeline_mode=pl.Buffered(1))
    return pl.pallas_call(
        _ffn_ln_kernel,
        grid=(m // tm,),
        in_specs=[
            pl.BlockSpec((tm, D_MODEL), lambda i: (i, 0)),
            pl.BlockSpec((D_MODEL, 2 * D_FF), const, **resident),
            pl.BlockSpec((D_FF, D_MODEL), const, **resident),
            pl.BlockSpec((1, D_MODEL), const),
            pl.BlockSpec((1, D_MODEL), const),
        ],
        out_specs=pl.BlockSpec((tm, D_MODEL), lambda i: (i, 0)),
        out_shape=jax.ShapeDtypeStruct((m, D_MODEL), F32),
        compiler_params=_params("parallel"),
        name="ffn_ln",
    )(x, w_up, w_down, g, b)


def _inproj_prompt_kernel(h_ref, wqt_ref, wk_ref, wkt_ref, wvt_ref, wft_ref, fb_ref, wc_ref, wu_ref, wz_ref, wab_ref,
                          qt_ref, kb_ref, kt_ref, vt_ref, vtb_ref, lft_ref, qn_ref, kn_ref, c_ref, u_ref, z_ref, ab_ref):
    hb = h_ref[...].astype(BF16)
    tm = hb.shape[0]
    head_sq = lambda a: jnp.sum((a * a).reshape(A_HEADS, A_HEAD_DIM, tm), axis=1)
    qt = _dot_nt(wqt_ref[...], hb).astype(BF16)
    qt_ref[0] = qt
    qn_ref[0] = head_sq(qt.astype(F32))
    kb_ref[0] = _dot(hb, wk_ref[...]).astype(BF16)
    kt = _dot_nt(wkt_ref[...], hb)
    kt_ref[0] = kt
    kn_ref[0] = head_sq(kt)
    vt = _dot_nt(wvt_ref[...], hb)
    vt_ref[0] = vt
    vtb_ref[0] = vt.astype(BF16)
    lft_ref[0] = _log_sigmoid(_dot_nt(wft_ref[...], hb) + fb_ref[...])
    c_ref[0] = _dot(hb, wc_ref[...])
    u_ref[0] = _dot(hb, wu_ref[...])
    z_ref[0] = _dot(hb, wz_ref[...])
    ab_ref[0] = _dot(hb, wab_ref[...])


def inproj_prompt(h, wqt, wk, wkt, wvt, wft, fbias, wc, wu, wz, wab, *, batch, seq, tm):
    m = h.shape[0]
    nt = seq // tm
    const = lambda i: (0, 0)
    tok3 = lambda i: (i // nt, i % nt, 0)
    feat3 = lambda i: (i // nt, 0, i % nt)
    weights = (wqt, wk, wkt, wvt, wft, fbias, wc, wu, wz, wab)
    tok_outs = ((A_WIDTH, BF16), (3 * C_WIDTH, F32), (B_WIDTH, F32), (C_WIDTH, F32), (LANES, F32))
    feat_outs = ((A_WIDTH, BF16), (A_WIDTH, F32), (A_WIDTH, F32), (A_WIDTH, BF16), (A_HEADS, F32),
                 (A_HEADS, F32), (A_HEADS, F32))
    specs = ([(feat_outs[0], True), (tok_outs[0], False)] + [(f, True) for f in feat_outs[1:]]
             + [(t, False) for t in tok_outs[1:]])
    return pl.pallas_call(
        _inproj_prompt_kernel,
        grid=(m // tm,),
        in_specs=[pl.BlockSpec((tm, D_MODEL), lambda i: (i, 0))] + [pl.BlockSpec(w.shape, const) for w in weights],
        out_specs=[pl.BlockSpec((1, n, tm), feat3) if feat else pl.BlockSpec((1, tm, n), tok3)
                   for (n, _), feat in specs],
        out_shape=[jax.ShapeDtypeStruct((batch, n, seq) if feat else (batch, seq, n), dt)
                   for (n, dt), feat in specs],
        compiler_params=_params("parallel"),
        name="inproj_prompt",
    )(h, *weights)


def _matmul_kernel(x_ref, w_ref, o_ref):
    o_ref[...] = _dot(x_ref[...].astype(BF16), w_ref[...])


def matmul_rows(x, w, *, tm):
    m, k = x.shape
    n = w.shape[1]
    return pl.pallas_call(
        _matmul_kernel,
        grid=(m // tm,),
        in_specs=[pl.BlockSpec((tm, k), lambda i: (i, 0)), pl.BlockSpec((k, n), lambda i: (0, 0))],
        out_specs=pl.BlockSpec((tm, n), lambda i: (i, 0)),
        out_shape=jax.ShapeDtypeStruct((m, n), F32),
        compiler_params=_params("parallel"),
        name="matmul_rows",
    )(x, w)


def _cumsum_lanes_kernel(x_ref, o_ref, r_ref, carry_ref, *, tb):
    @pl.when(pl.program_id(1) == 0)
    def _():
        carry_ref[...] = jnp.zeros_like(carry_ref)

    h = x_ref.shape[1]
    r = lax.broadcasted_iota(jnp.int32, (tb, tb), 0)
    c = lax.broadcasted_iota(jnp.int32, (tb, tb), 1)
    upper = (r <= c).astype(F32)
    out = _dot_hi(x_ref[0], upper) + carry_ref[...]
    carry_ref[...] = out[:, tb - 1:tb]
    r_ref[0] = out * LOG2E
    cols = jnp.concatenate([out * LOG2E, jnp.zeros((LANES - h, tb), F32)], axis=0).T
    for i in range(h):
        o_ref[0, i] = cols[:, i:i + 1]


def cumsum_lanes(x, *, tb):
    b, h, t = x.shape
    return pl.pallas_call(
        functools.partial(_cumsum_lanes_kernel, tb=tb),
        grid=(b, t // tb),
        in_specs=[pl.BlockSpec((1, h, tb), lambda i, j: (i, 0, j))],
        out_specs=[pl.BlockSpec((1, h, tb, 1), lambda i, j: (i, 0, j, 0)),
                   pl.BlockSpec((1, h, tb), lambda i, j: (i, 0, j))],
        out_shape=[jax.ShapeDtypeStruct((b, h, t, 1), F32), jax.ShapeDtypeStruct((b, h, t), F32)],
        scratch_shapes=[pltpu.VMEM((h, 1), F32)],
        compiler_params=_params("parallel", "arbitrary"),
        name="cumsum_lanes",
    )(x)


def _fox_prompt_kernel(fast_ref, qt_ref, k_ref, vt_ref, c_ref, cr_ref, qn_ref, kn_ref, o_ref, km_ref, *, tq):
    i = pl.program_id(2)
    plan = fast_ref[(pl.program_id(0) * pl.num_programs(1) + pl.program_id(1)) * pl.num_programs(2) + i]
    fast = plan & 1
    first = plan >> 1

    @pl.when(i == 0)
    def _():
        kb = k_ref[0]
        low = lax.broadcasted_iota(jnp.int32, kb.shape, 1) < A_HEAD_DIM
        zero = jnp.zeros_like(kb)
        km_ref[0] = jnp.where(low, kb, zero)
        km_ref[1] = jnp.where(low, zero, kb)

    qt = qt_ref[0]
    key = lax.broadcasted_iota(jnp.int32, (tq, tq), 0)
    qry = lax.broadcasted_iota(jnp.int32, (tq, tq), 1)
    causal = key <= qry

    def scores_of(j, masked):
        off = pl.multiple_of(j * tq, tq)
        out = []
        for hh in range(2):
            s = _dot(km_ref[hh, pl.ds(off, tq), :], qt) - c_ref[0, hh, pl.ds(off, tq), :]
            out.append(jnp.where(causal, s, NEG_BIG) if masked else s)
        return out, [vt_ref[0, hh * A_HEAD_DIM:(hh + 1) * A_HEAD_DIM, pl.ds(off, tq)] for hh in range(2)]

    def finish(a0, l0, a1, l1):
        o_ref[0] = jnp.concatenate([a0 / l0, a1 / l1], axis=0).T.astype(o_ref.dtype)

    @pl.when(fast != 0)
    def _():
        ref = [qn_ref[0, 0, hh:hh + 1, :] * kn_ref[0, 0, hh:hh + 1, 0:1] - cr_ref[0, 0, hh:hh + 1, :]
               for hh in range(2)]

        def step(j, carry, masked):
            scores, vt = scores_of(j, masked)
            new = []
            for hh in range(2):
                l, acc = carry[hh]
                p = jnp.exp2(scores[hh] - ref[hh])
                new.append((l + jnp.sum(p, axis=0, keepdims=True), acc + _dot(vt[hh], p.astype(BF16))))
            return tuple(new)

        one = (jnp.zeros((1, tq), F32), jnp.zeros((A_HEAD_DIM, tq), F32))
        carry = lax.fori_loop(first, i, lambda j, c: step(j, c, False), (one, one))
        hq = tq // 2
        lo_off = pl.multiple_of(i * tq, tq)
        hi_off = pl.multiple_of(i * tq + hq, hq)
        tri_lo = lax.broadcasted_iota(jnp.int32, (hq, tq), 0) <= lax.broadcasted_iota(jnp.int32, (hq, tq), 1)
        tri_hi = lax.broadcasted_iota(jnp.int32, (hq, hq), 0) <= lax.broadcasted_iota(jnp.int32, (hq, hq), 1)
        done = []
        for hh in range(2):
            l, acc = carry[hh]
            rows_v = slice(hh * A_HEAD_DIM, (hh + 1) * A_HEAD_DIM)
            s_lo = _dot(km_ref[hh, pl.ds(lo_off, hq), :], qt) - c_ref[0, hh, pl.ds(lo_off, hq), :]
            s_hi = _dot(km_ref[hh, pl.ds(hi_off, hq), :], qt[:, hq:]) - c_ref[0, hh, pl.ds(hi_off, hq), :]
            p_lo = jnp.exp2(jnp.where(tri_lo, s_lo, NEG_BIG) - ref[hh])
            p_hi = jnp.exp2(jnp.where(tri_hi, s_hi, NEG_BIG) - ref[hh][:, hq:])
            l_hi = jnp.sum(p_hi, axis=0, keepdims=True)
            a_hi = _dot(vt_ref[0, rows_v, pl.ds(hi_off, hq)], p_hi.astype(BF16))
            l = l + jnp.sum(p_lo, axis=0, keepdims=True) + jnp.concatenate([jnp.zeros_like(l_hi), l_hi], axis=1)
            acc = (acc + _dot(vt_ref[0, rows_v, pl.ds(lo_off, hq)], p_lo.astype(BF16))
                   + jnp.concatenate([jnp.zeros_like(a_hi), a_hi], axis=1))
            done.append((l, acc))
        (l0, a0), (l1, a1) = done
        finish(a0, l0, a1, l1)

    @pl.when(fast == 0)
    def _():
        def step(j, carry, masked):
            scores, vt = scores_of(j, masked)
            new = []
            for hh in range(2):
                m, l, acc = carry[hh]
                m_new = jnp.maximum(m, jnp.max(scores[hh], axis=0, keepdims=True))
                alpha = jnp.exp2(m - m_new)
                p = jnp.exp2(scores[hh] - m_new)
                new.append((m_new, alpha * l + jnp.sum(p, axis=0, keepdims=True),
                            alpha * acc + _dot(vt[hh], p.astype(BF16))))
            return tuple(new)

        one = (jnp.full((1, tq), NEG_BIG, F32), jnp.zeros((1, tq), F32), jnp.zeros((A_HEAD_DIM, tq), F32))
        carry = lax.fori_loop(0, i, lambda j, c: step(j, c, False), (one, one))
        (_, l0, a0), (_, l1, a1) = step(i, carry, True)
        finish(a0, l0, a1, l1)


FOX_FAST_BOUND = 50.0
FOX_ZERO_EXP2 = -152.0


def fox_prompt(qt, k, vt, c_col, c_row, qn2, kn2, *, tq):
    b, t, _ = k.shape
    pairs = A_HEADS // 2
    nq = t // tq
    qn = jnp.sqrt(qn2)
    kn = jnp.sqrt(jnp.max(kn2, axis=-1)) * 1.01 + 1e-6
    fast = jnp.max((qn * kn[..., None]).reshape(b, pairs, 2, nq, tq), axis=(2, 4)) <= FOX_FAST_BOUND
    c_blocks = c_row.reshape(b, pairs, 2, nq, tq)
    gap = c_blocks[..., None, :, tq - 1] - c_blocks[..., :, None, 0]
    dead = jnp.all(gap >= -FOX_ZERO_EXP2, axis=2) & (jnp.arange(nq)[None, :] < jnp.arange(nq)[:, None])
    first = jnp.sum(jnp.cumprod(dead.astype(jnp.int32), axis=-1), axis=-1)
    plan = fast.astype(jnp.int32) + 2 * jnp.where(fast, first, 0)
    kn_l = jnp.broadcast_to(kn.reshape(b, pairs, 2, 1), (b, pairs, 2, LANES))
    pair_row = lambda bi, hp, i, f: (bi, hp, 0, i)
    grid_spec = pltpu.PrefetchScalarGridSpec(
        num_scalar_prefetch=1,
        grid=(b, pairs, nq),
        in_specs=[
            pl.BlockSpec((1, LANES, tq), lambda bi, hp, i, f: (bi, hp, i)),
            pl.BlockSpec((1, t, LANES), lambda bi, hp, i, f: (bi, 0, hp)),
            pl.BlockSpec((1, LANES, t), lambda bi, hp, i, f: (bi, hp, 0)),
            pl.BlockSpec((1, 2, t, 1), lambda bi, hp, i, f: (bi, hp, 0, 0)),
            pl.BlockSpec((1, 1, 2, tq), pair_row),
            pl.BlockSpec((1, 1, 2, tq), pair_row),
            pl.BlockSpec((1, 1, 2, LANES), lambda bi, hp, i, f: (bi, hp, 0, 0)),
        ],
        out_specs=pl.BlockSpec((1, tq, LANES), lambda bi, hp, i, f: (bi, i, hp)),
        scratch_shapes=[pltpu.VMEM((2, t, LANES), BF16)],
    )
    return pl.pallas_call(
        functools.partial(_fox_prompt_kernel, tq=tq),
        grid_spec=grid_spec,
        out_shape=jax.ShapeDtypeStruct((b, t, A_WIDTH), BF16),
        compiler_params=_params("parallel", "parallel", "arbitrary"),
        name="fox_prompt",
    )(plan.reshape(-1), qt, k, vt, c_col, c_row.reshape(b, pairs, 2, t),
      qn.reshape(b, pairs, 2, t), kn_l)


def _head_mask():
    hrow = lax.broadcasted_iota(jnp.int32, (A_HEADS, A_WIDTH), 0)
    hcol = lax.broadcasted_iota(jnp.int32, (A_HEADS, A_WIDTH), 1) // A_HEAD_DIM
    return hrow == hcol


def _block_diag_queries(q_ref, n_q):
    mask = _head_mask()
    return jnp.concatenate(
        [jnp.where(mask, jnp.broadcast_to(q_ref[0, qi:qi + 1, :], (A_HEADS, A_WIDTH)), 0.0) for qi in range(n_q)],
        axis=0)


def _fox_scores_kernel(pt_ref, q_ref, *rest, n_q, group):
    kp_refs, lfp_refs = rest[:group], rest[group:2 * group]
    s_ref, qbd_ref, carry_ref = rest[2 * group:]

    @pl.when(pl.program_id(1) == 0)
    def _():
        carry_ref[...] = jnp.zeros_like(carry_ref)
        qbd_ref[...] = _block_diag_queries(q_ref, n_q).astype(BF16)

    qbd = qbd_ref[...]
    r = lax.broadcasted_iota(jnp.int32, (PAGE_SIZE, PAGE_SIZE), 0)
    c = lax.broadcasted_iota(jnp.int32, (PAGE_SIZE, PAGE_SIZE), 1)
    after = (r > c).astype(F32)
    carry = carry_ref[...]
    for g in range(group):
        lf = lfp_refs[g][...]
        bias = _dot_hi(lf, after) + carry
        carry = carry + jnp.sum(lf, axis=-1, keepdims=True)
        s = _dot(qbd, kp_refs[g][...].astype(BF16)) + jnp.concatenate([bias] * n_q, axis=0)
        s_ref[0, :, (group - 1 - g) * PAGE_SIZE:(group - g) * PAGE_SIZE] = s
    carry_ref[...] = carry


def _fox_values_kernel(vpage_ref, pos_ref, nlive_ref, q_ref, kn_ref, vn_ref, lfn_ref, s_ref, mpast_ref, *rest,
                       n_q, group, n_pages):
    vp_refs = rest[:group]
    o_ref, m_ref, l_ref, acc_ref = rest[group:]
    b = pl.program_id(0)
    p = pl.program_id(1)
    rows = n_q * A_HEADS

    @pl.when(p == 0)
    def _():
        qbd = _block_diag_queries(q_ref, n_q)
        lfn = lfn_ref[0]
        qidx = lax.broadcasted_iota(jnp.int32, (rows, 1), 0) // A_HEADS
        s_new = []
        cum = jnp.zeros((A_HEADS, 1), F32)
        for ki in range(n_q):
            cum = cum + lfn[:, ki:ki + 1]
            sk = jnp.sum(qbd * kn_ref[0, ki:ki + 1, :], axis=-1, keepdims=True)
            sk = sk - jnp.concatenate([cum] * n_q, axis=0)
            s_new.append(jnp.where(qidx >= ki, sk, NEG_BIG))
        m = mpast_ref[0]
        for sk in s_new:
            m = jnp.maximum(m, sk)
        l = jnp.zeros((rows, 1), F32)
        acc = jnp.zeros((rows, A_WIDTH), F32)
        for ki, sk in enumerate(s_new):
            pk = jnp.exp(sk - m)
            l = l + pk
            acc = acc + pk * vn_ref[0, ki:ki + 1, :]
        m_ref[...] = m
        l_ref[...] = l
        acc_ref[...] = acc

    @pl.when(p * group < nlive_ref[b])
    def _():
        m = m_ref[...]
        l = l_ref[...]
        pv = None
        for g in range(group):
            pos = pos_ref[b * n_pages + p * group + g]
            s = s_ref[0, :, pl.ds(pl.multiple_of(pos * PAGE_SIZE, PAGE_SIZE), PAGE_SIZE)]
            pr = jnp.exp(s - m)
            l = l + jnp.sum(pr, axis=-1, keepdims=True)
            t = _dot_nt(pr.astype(BF16), vp_refs[g][...].astype(BF16))
            pv = t if pv is None else pv + t
        l_ref[...] = l
        acc_ref[...] += pv

    @pl.when(p == pl.num_programs(1) - 1)
    def _():
        mask = _head_mask()
        acc = acc_ref[...] / l_ref[...]
        for qi in range(n_q):
            blk = jnp.where(mask, acc[qi * A_HEADS:(qi + 1) * A_HEADS, :], 0.0)
            o_ref[0, qi:qi + 1, :] = jnp.sum(blk, axis=0, keepdims=True)


FOX_ZERO_EXP = -106.0


def fox_sample_two_pass(page_table, q, k_new, v_new, lf_new_t, cache_kt, cache_vt, cache_lft, *, layer, group):
    bd, n_q, _ = q.shape
    n_pages = page_table.shape[1]
    rows = n_q * A_HEADS
    n_groups = n_pages // group
    assert n_pages % group == 0
    seq3 = lambda b, p, *_: (b, 0, 0)

    def page(g):
        return lambda b, p, pt_ref: (layer, pt_ref[b * n_pages + (n_pages - 1 - (p * group + g))], 0, 0)

    scores = pl.pallas_call(
        functools.partial(_fox_scores_kernel, n_q=n_q, group=group),
        grid_spec=pltpu.PrefetchScalarGridSpec(
            num_scalar_prefetch=1,
            grid=(bd, n_groups),
            in_specs=[pl.BlockSpec((1, n_q, A_WIDTH), seq3)]
            + [pl.BlockSpec((None, None, A_WIDTH, PAGE_SIZE), page(g)) for g in range(group)]
            + [pl.BlockSpec((None, None, A_HEADS, PAGE_SIZE), page(g)) for g in range(group)],
            out_specs=pl.BlockSpec((1, rows, group * PAGE_SIZE), lambda b, p, pt_ref: (b, 0, n_groups - 1 - p)),
            scratch_shapes=[pltpu.VMEM((rows, A_WIDTH), BF16), pltpu.VMEM((A_HEADS, 1), F32)],
        ),
        out_shape=jax.ShapeDtypeStruct((bd, rows, n_pages * PAGE_SIZE), F32),
        compiler_params=_params("parallel", "arbitrary"),
        name="fox_sample_scores",
    )(page_table.reshape(-1), q, *([cache_kt] * group + [cache_lft] * group))

    page_max = jnp.max(scores.reshape(bd, rows, n_pages, PAGE_SIZE), axis=-1)
    m_past = jnp.max(page_max, axis=-1, keepdims=True)
    live = jnp.any(page_max - m_past > FOX_ZERO_EXP, axis=1)
    n_live = jnp.sum(live, axis=-1).astype(jnp.int32)
    newest_first = jnp.arange(n_pages - 1, -1, -1, dtype=jnp.int32)
    rank = jnp.where(live, 0, n_pages) + newest_first[None, :]
    pos = jnp.argsort(rank, axis=-1).astype(jnp.int32)
    slot = jnp.minimum(jnp.arange(n_pages, dtype=jnp.int32)[None, :], n_live[:, None] - 1)
    pos = jnp.take_along_axis(pos, slot, axis=-1)
    vpage = jnp.take_along_axis(page_table, pos, axis=-1)

    def vpage_map(g):
        return lambda b, p, vp_ref, pos_ref, nl_ref: (layer, vp_ref[b * n_pages + p * group + g], 0, 0)

    return pl.pallas_call(
        functools.partial(_fox_values_kernel, n_q=n_q, group=group, n_pages=n_pages),
        grid_spec=pltpu.PrefetchScalarGridSpec(
            num_scalar_prefetch=3,
            grid=(bd, n_groups),
            in_specs=[
                pl.BlockSpec((1, n_q, A_WIDTH), seq3),
                pl.BlockSpec((1, n_q, A_WIDTH), seq3),
                pl.BlockSpec((1, n_q, A_WIDTH), seq3),
                pl.BlockSpec((1, A_HEADS, n_q), seq3),
                pl.BlockSpec((1, rows, n_pages * PAGE_SIZE), seq3),
                pl.BlockSpec((1, rows, 1), seq3),
            ] + [pl.BlockSpec((None, None, A_WIDTH, PAGE_SIZE), vpage_map(g)) for g in range(group)],
            out_specs=pl.BlockSpec((1, n_q, A_WIDTH), seq3),
            scratch_shapes=[pltpu.VMEM((rows, 1), F32), pltpu.VMEM((rows, 1), F32), pltpu.VMEM((rows, A_WIDTH), F32)],
        ),
        out_shape=jax.ShapeDtypeStruct((bd, n_q, A_WIDTH), F32),
        compiler_params=_params("parallel", "arbitrary"),
        name="fox_sample_values",
    )(vpage.reshape(-1), pos.reshape(-1), n_live, q, k_new, v_new, lf_new_t, scores, m_past,
      *([cache_vt] * group))


def _pool_kernel(u_ref, pre_ref, w_ref, sc_ref, o_ref, ext_ref, *, tt, n_valid):
    halo = POOL_BUF + 1
    j = pl.program_id(1)

    @pl.when(j == 0)
    def _():
        ext_ref[0:halo, :] = pre_ref[0]

    @pl.when(j > 0)
    def _():
        ext_ref[0:halo, :] = ext_ref[tt:tt + halo, :]

    u = u_ref[0]
    ext_ref[halo:halo + tt, :] = u
    pos = j * tt + lax.broadcasted_iota(jnp.int32, (tt, 1), 0) + (1 + n_valid)
    outs = []
    for gi, w in enumerate(POOL_WINDOWS):
        sl = slice(gi * POOL_GROUP_DIM, (gi + 1) * POOL_GROUP_DIM)
        tot = u[:, sl]
        for k in range(1, w):
            tot = tot + ext_ref[halo - k:halo - k + tt, sl]
        cnt = jnp.minimum(pos, w).astype(F32)
        d = tot / cnt - u[:, sl]
        outs.append(_dot(d.astype(BF16), w_ref[gi]))
    o_ref[0] = (jnp.concatenate(outs, axis=-1) * sc_ref[...]).astype(o_ref.dtype)


def pool_mix(u, prefix16, pool_w, pool_scale, *, tt, n_valid, out_dtype):
    b, t, _ = u.shape
    halo = POOL_BUF + 1
    return pl.pallas_call(
        functools.partial(_pool_kernel, tt=tt, n_valid=n_valid),
        grid=(b, t // tt),
        in_specs=[
            pl.BlockSpec((1, tt, B_WIDTH), lambda i, j: (i, j, 0)),
            pl.BlockSpec((1, halo, B_WIDTH), lambda i, j: (i, 0, 0)),
            pl.BlockSpec((len(POOL_WINDOWS), POOL_GROUP_DIM, POOL_GROUP_DIM), lambda i, j: (0, 0, 0)),
            pl.BlockSpec((1, B_WIDTH), lambda i, j: (0, 0)),
        ],
        out_specs=pl.BlockSpec((1, tt, B_WIDTH), lambda i, j: (i, j, 0)),
        out_shape=jax.ShapeDtypeStruct((b, t, B_WIDTH), out_dtype),
        scratch_shapes=[pltpu.VMEM((tt + 2 * halo, B_WIDTH), F32)],
        compiler_params=_params("parallel", "arbitrary"),
        name="pool_mix",
    )(u, prefix16, pool_w, pool_scale)


def _gdn_kernel(x_ref, ab_ref, z_ref, pre_ref, s0_ref, cw_ref, alog_ref, dtb_ref, nw_ref,
                o_ref, s_out_ref, ext_ref, s_ref, *, chunk, n_chunks, t_valid):
    j = pl.program_id(1)
    halo = 8

    @pl.when(j == 0)
    def _():
        ext_ref[0:halo, :] = pre_ref[0]
        s_ref[...] = s0_ref[0]

    span = chunk * n_chunks

    @pl.when(j > 0)
    def _():
        ext_ref[0:halo, :] = ext_ref[span:span + halo, :]

    ext_ref[halo:halo + span, :] = x_ref[0]
    y_all = ext_ref[halo:halo + span, :] * cw_ref[CONV_WIDTH - 1:CONV_WIDTH, :]
    for k in range(1, CONV_WIDTH):
        y_all = y_all + ext_ref[halo - k:halo - k + span, :] * cw_ref[CONV_WIDTH - 1 - k:CONV_WIDTH - k, :]
    y_all = _silu(y_all)

    ab = ab_ref[0]
    valid = (j * span + lax.broadcasted_iota(jnp.int32, (span, 1), 0)) < t_valid
    g_all = jnp.where(valid, -jnp.exp(alog_ref[...]) * _softplus(ab + dtb_ref[...]), 0.0)
    beta_all = jnp.where(valid, _sigmoid(ab), 0.0)
    z_all = z_ref[0]
    r = lax.broadcasted_iota(jnp.int32, (chunk, chunk), 0)
    c = lax.broadcasted_iota(jnp.int32, (chunk, chunk), 1)
    tri = (r >= c).astype(F32)

    n = C_HEADS * chunk
    rows = [slice(h * chunk, (h + 1) * chunk) for h in range(C_HEADS)]
    row_head = lax.broadcasted_iota(jnp.int32, (n, LANES), 0) // chunk
    lane = lax.broadcasted_iota(jnp.int32, (n, LANES), 1)
    pick = lambda a, off: jnp.where(lane == row_head + off, jnp.concatenate([a] * C_HEADS, axis=0), 0.0)
    rr = lax.broadcasted_iota(jnp.int32, (n, n), 0)
    cc = lax.broadcasted_iota(jnp.int32, (n, n), 1)
    same = (rr // chunk) == (cc // chunk)

    def setup(ci):
        tok = slice(ci * chunk, (ci + 1) * chunk)
        y = y_all[tok]
        gcum = _dot_hi(tri, g_all[tok])
        heads = lambda off: jnp.concatenate(
            [y[:, off + h * C_HEAD_DIM:off + (h + 1) * C_HEAD_DIM] for h in range(C_HEADS)], axis=0)
        qs, ks, vs = heads(0), heads(C_WIDTH), heads(2 * C_WIDTH)
        qs = qs * lax.rsqrt(jnp.sum(qs * qs, axis=-1, keepdims=True) + L2_EPS) * (C_HEAD_DIM ** -0.5)
        ks = ks * lax.rsqrt(jnp.sum(ks * ks, axis=-1, keepdims=True) + L2_EPS)
        gd = pick(gcum, 0)
        gc = jnp.sum(gd, axis=-1, keepdims=True)
        gr = _dot_nt(jnp.ones((8, LANES), F32), gd, precision=HIGHEST)[0:1, :]
        beta = jnp.sum(pick(beta_all[tok], C_HEADS), axis=-1, keepdims=True)
        g_last = jnp.sum(pick(jnp.broadcast_to(gcum[chunk - 1:chunk, :], (chunk, LANES)), 0),
                         axis=-1, keepdims=True)
        gamma = jnp.exp(jnp.where(same & (rr >= cc), gc - gr, -jnp.inf))
        eg = jnp.exp(gc)
        ksb = ks.astype(BF16)
        return dict(
            nmat=jnp.where(same & (rr > cc), beta * _dot_nt(ksb, ksb) * gamma, 0.0),
            bv=_split(beta * vs), bk=_split(beta * ks * eg),
            qk=(_dot_nt(qs.astype(BF16), ksb) * gamma).astype(BF16),
            k_dec=ks * jnp.exp(g_last - gc), q_dec=qs * eg,
            decay=[jnp.exp(gcum[chunk - 1:chunk, h:h + 1]) for h in range(C_HEADS)],
            z=jnp.concatenate([z_all[tok, h * C_HEAD_DIM:(h + 1) * C_HEAD_DIM] for h in range(C_HEADS)], axis=0))

    parts = [setup(ci) for ci in range(n_chunks)]
    inv = [(rr == cc).astype(F32) - p["nmat"] for p in parts]
    pw = [_split(p["nmat"]) for p in parts]
    for _ in range(max(1, (chunk - 1).bit_length()) - 1):
        pw = [_split(_dot(x[0], x[0])) for x in pw]
        inv = [a + _mm_split(_split(a), x) for a, x in zip(inv, pw)]
    inv = [_split(a) for a in inv]
    for p, a in zip(parts, inv):
        p["u"] = _mm_split(a, p["bv"])
        p["w"] = _mm_split(a, p["bk"])
    state = [s_ref[h] for h in range(C_HEADS)]
    for ci, p in enumerate(parts):
        state_b = [s.astype(BF16) for s in state]
        v_new = jnp.concatenate(
            [p["u"][rows[h]] - _dot(p["w"][rows[h]].astype(BF16), state_b[h]) for h in range(C_HEADS)], axis=0)
        o = jnp.concatenate([_dot(p["q_dec"][rows[h]].astype(BF16), state_b[h]) for h in range(C_HEADS)], axis=0)
        o = o + _dot(p["qk"], v_new.astype(BF16))
        state = [state[h] * p["decay"][h] + lax.dot_general(
            p["k_dec"][rows[h]].astype(BF16), v_new[rows[h]].astype(BF16), (((0,), (0,)), ((), ())),
            preferred_element_type=F32) for h in range(C_HEADS)]
        o = o * lax.rsqrt(jnp.mean(o * o, axis=-1, keepdims=True) + RMS_EPS) * nw_ref[...]
        o = o * _silu(p["z"])
        o_ref[0, ci * chunk:(ci + 1) * chunk, :] = jnp.concatenate(
            [o[rows[h]] for h in range(C_HEADS)], axis=-1).astype(o_ref.dtype)
    for h in range(C_HEADS):
        s_ref[h] = state[h]

    @pl.when(j == pl.num_programs(1) - 1)
    def _():
        s_out_ref[0] = s_ref[...]


def gdn_mixer(x, ab, z, prefix8, s0, conv_w, a_log_row, dt_bias_row, norm_w, *, chunk, n_chunks, t_valid,
              out_dtype):
    b, t, _ = x.shape
    span = chunk * n_chunks
    row = lambda i, j: (0, 0)
    tok = lambda i, j: (i, j, 0)
    seq = lambda i, j: (i, 0, 0)
    return pl.pallas_call(
        functools.partial(_gdn_kernel, chunk=chunk, n_chunks=n_chunks, t_valid=t_valid),
        grid=(b, t // span),
        in_specs=[
            pl.BlockSpec((1, span, 3 * C_WIDTH), tok),
            pl.BlockSpec((1, span, LANES), tok),
            pl.BlockSpec((1, span, C_WIDTH), tok),
            pl.BlockSpec((1, 8, 3 * C_WIDTH), seq),
            pl.BlockSpec((1, C_HEADS, C_HEAD_DIM, C_HEAD_DIM), lambda i, j: (i, 0, 0, 0)),
            pl.BlockSpec((CONV_WIDTH, 3 * C_WIDTH), row),
            pl.BlockSpec((1, LANES), row),
            pl.BlockSpec((1, LANES), row),
            pl.BlockSpec((1, C_HEAD_DIM), row),
        ],
        out_specs=[
            pl.BlockSpec((1, span, C_WIDTH), tok),
            pl.BlockSpec((1, C_HEADS, C_HEAD_DIM, C_HEAD_DIM), lambda i, j: (i, 0, 0, 0)),
        ],
        out_shape=[
            jax.ShapeDtypeStruct((b, t, C_WIDTH), out_dtype),
            jax.ShapeDtypeStruct((b, C_HEADS, C_HEAD_DIM, C_HEAD_DIM), F32),
        ],
        scratch_shapes=[
            pltpu.VMEM((span + 16, 3 * C_WIDTH), F32),
            pltpu.VMEM((C_HEADS, C_HEAD_DIM, C_HEAD_DIM), F32),
        ],
        compiler_params=_params("parallel", "arbitrary"),
        name="gdn_mixer",
    )(x, ab, z, prefix8, s0, conv_w, a_log_row, dt_bias_row, norm_w)


def _merge_kernel(h_ref, oa_ref, ob_ref, oc_ref, wg_ref, wa_ref, wb_ref, wc_ref, wo_ref, g_ref, b_ref, o_ref):
    h = h_ref[...]
    hb = h.astype(BF16)
    merged = None
    for i, (x_ref, w_ref) in enumerate(((oa_ref, wa_ref), (ob_ref, wb_ref), (oc_ref, wc_ref))):
        gate = _sigmoid(_dot(hb, wg_ref[:, i * D_MODEL:(i + 1) * D_MODEL]))
        term = gate * _dot(x_ref[...].astype(BF16), w_ref[...])
        merged = term if merged is None else merged + term
    y = DN_ALPHA * h + _dot(merged.astype(BF16), wo_ref[...])
    o_ref[...] = _layer_norm(y, g_ref[...], b_ref[...])


def merge_ln(h, oa, ob, oc, wg, wa, wb, wc, wo, g, b, *, tm):
    m = h.shape[0]
    const = lambda i: (0, 0)
    rows = lambda i: (i, 0)
    return pl.pallas_call(
        _merge_kernel,
        grid=(m // tm,),
        in_specs=[
            pl.BlockSpec((tm, D_MODEL), rows),
            pl.BlockSpec((tm, A_WIDTH), rows),
            pl.BlockSpec((tm, B_WIDTH), rows),
            pl.BlockSpec((tm, C_WIDTH), rows),
            pl.BlockSpec((D_MODEL, N_BRANCH * D_MODEL), const),
            pl.BlockSpec((A_WIDTH, D_MODEL), const),
            pl.BlockSpec((B_WIDTH, D_MODEL), const),
            pl.BlockSpec((C_WIDTH, D_MODEL), const),
            pl.BlockSpec((D_MODEL, D_MODEL), const),
            pl.BlockSpec((1, D_MODEL), const),
            pl.BlockSpec((1, D_MODEL), const),
        ],
        out_specs=pl.BlockSpec((tm, D_MODEL), rows),
        out_shape=jax.ShapeDtypeStruct((m, D_MODEL), F32),
        compiler_params=_params("parallel"),
        name="merge_ln",
    )(h, oa, ob, oc, wg, wa, wb, wc, wo, g, b)


def _split_w_in(w_in_l):
    o = 0
    parts = {}
    for name, n in (("q", A_WIDTH), ("k", A_WIDTH), ("v", A_WIDTH), ("f", A_HEADS), ("ub", B_WIDTH),
                    ("qkvc", 3 * C_WIDTH), ("a", C_HEADS), ("b", C_HEADS), ("z", C_WIDTH),
                    ("gate", N_BRANCH * D_MODEL)):
        parts[name] = w_in_l[:, o:o + n]
        o += n
    return parts


def kernel(x_prompt, x_sample, cache_k, cache_v, cache_logf, page_table, state_pool, state_conv, state_ssm,
           w_in, fox_f_bias, gdn_conv_w, gdn_a_log, gdn_dt_bias, gdn_norm_w, pool_w, pool_scale,
           w_branch_a, w_branch_b, w_branch_c, w_out, ffn_w_in, ffn_w_out, ln_g, ln_b):
    bp, seq, _ = x_prompt.shape
    bd, n_q, _ = x_sample.shape
    mp, ms = bp * seq, bd * n_q
    tm_p = 512
    tm_s = ms

    n_pool = cache_k.shape[1]
    cache_kt = cache_k.transpose(0, 1, 3, 4, 2).reshape(DEPTH, n_pool, A_WIDTH, PAGE_SIZE)
    cache_vt = cache_v.transpose(0, 1, 3, 4, 2).reshape(DEPTH, n_pool, A_WIDTH, PAGE_SIZE)
    cache_lft = cache_logf.transpose(0, 1, 3, 2)

    yp = x_prompt.reshape(mp, D_MODEL)
    ys = x_sample.reshape(ms, D_MODEL)
    outs_p = ([], [], [], [], [], [])
    outs_s = ([], [], [], [], [], [])
    zeros_ab = jnp.zeros((D_MODEL, LANES - 2 * C_HEADS), F32)
    for l in range(DEPTH):
        wp = _split_w_in(w_in[l])
        wq = (wp["q"] * (A_HEAD_DIM ** -0.5)).astype(BF16)
        w_ab = jnp.concatenate([wp["a"], wp["b"], zeros_ab], axis=1)
        w_rest = jnp.concatenate([wp["qkvc"], wp["ub"], wp["z"], w_ab], axis=1).astype(BF16)
        o_qkvc, o_ub, o_z, o_ab = 0, 3 * C_WIDTH, 3 * C_WIDTH + B_WIDTH, 3 * C_WIDTH + B_WIDTH + C_WIDTH
        w_f_pad = jnp.concatenate([wp["f"], jnp.zeros((D_MODEL, LANES - A_HEADS), F32)], axis=1)
        w_sample = jnp.concatenate([wq.astype(F32), wp["k"], wp["v"], w_f_pad], axis=1).astype(BF16)
        w_gate = wp["gate"].astype(BF16)
        ffn_up = ffn_w_in[l].astype(BF16)
        ffn_dn = ffn_w_out[l].astype(BF16)
        lg = ln_g[l].reshape(3, 1, D_MODEL)
        lb = ln_b[l].reshape(3, 1, D_MODEL)
        fb = fox_f_bias[l]
        a_log_row = jnp.zeros((1, LANES), F32).at[0, :C_HEADS].set(gdn_a_log[l])
        dt_row = jnp.zeros((1, LANES), F32).at[0, :C_HEADS].set(gdn_dt_bias[l])
        norm_w = gdn_norm_w[l].reshape(1, C_HEAD_DIM)
        pw = pool_w[l].astype(BF16)
        psc = pool_scale[l].reshape(1, B_WIDTH)
        w_merge = (w_gate, w_branch_a[l].astype(BF16), w_branch_b[l].astype(BF16),
                   w_branch_c[l].astype(BF16), w_out[l].astype(BF16))

        hp = ffn_ln(yp, ffn_up[0], ffn_dn[0], lg[0], lb[0], tm=tm_p)
        hs = ffn_ln(ys, ffn_up[0], ffn_dn[0], lg[0], lb[0], tm=tm_s)

        wqt = (wp["q"] * (A_HEAD_DIM ** -0.5 * LOG2E)).T.astype(BF16)
        qt_p, kb_p, kt_p, vt_p, vtb_p, lft_p, qn2_p, kn2_p, qkvc_p, ub_p, z_p, ab_p = inproj_prompt(
            hp, wqt, wp["k"].astype(BF16), wp["k"].T.astype(BF16), wp["v"].T.astype(BF16), wp["f"].T.astype(BF16),
            fb.reshape(A_HEADS, 1), wp["qkvc"].astype(BF16), wp["ub"].astype(BF16), wp["z"].astype(BF16),
            w_ab.astype(BF16), batch=bp, seq=seq, tm=256)
        ccol_p, crow_p = cumsum_lanes(lft_p, tb=512)
        oa_p = fox_prompt(qt_p, kb_p, vtb_p, ccol_p, crow_p, qn2_p, kn2_p, tq=1024).reshape(mp, A_WIDTH)
        ob_p = pool_mix(ub_p, jnp.zeros((bp, POOL_BUF + 1, B_WIDTH), F32), pw, psc,
                        tt=512, n_valid=0, out_dtype=BF16).reshape(mp, B_WIDTH)
        oc_p, ssm_p = gdn_mixer(qkvc_p, ab_p, z_p,
                                jnp.zeros((bp, 8, 3 * C_WIDTH), F32),
                                jnp.zeros((bp, C_HEADS, C_HEAD_DIM, C_HEAD_DIM), F32),
                                gdn_conv_w[l], a_log_row, dt_row, norm_w, chunk=64, n_chunks=4, t_valid=seq,
                                out_dtype=BF16)
        yp = merge_ln(hp, oa_p, ob_p, oc_p.reshape(mp, C_WIDTH), *w_merge, lg[1], lb[1], tm=tm_p)
        yp = ffn_ln(yp, ffn_up[1], ffn_dn[1], lg[2], lb[2], tm=tm_p)
        k_p = kt_p.reshape(bp, A_HEADS, A_HEAD_DIM, seq).transpose(0, 3, 1, 2)
        v_p = vt_p.reshape(bp, A_HEADS, A_HEAD_DIM, seq).transpose(0, 3, 1, 2)
        for lst, a in zip(outs_p, (k_p, v_p, lft_p.transpose(0, 2, 1), ub_p[:, seq - POOL_BUF:],
                                   qkvc_p[:, seq - CONV_BUF:], ssm_p)):
            lst.append(a)

        proj_s = matmul_rows(hs, w_sample, tm=tm_s)
        rest_s = matmul_rows(hs, w_rest, tm=tm_s).reshape(bd, n_q, -1)
        q_s = proj_s[:, :A_WIDTH].reshape(bd, n_q, A_WIDTH)
        k_s = proj_s[:, A_WIDTH:2 * A_WIDTH].reshape(bd, n_q, A_WIDTH)
        v_s = proj_s[:, 2 * A_WIDTH:3 * A_WIDTH].reshape(bd, n_q, A_WIDTH)
        lf_s = jax.nn.log_sigmoid(proj_s[:, 3 * A_WIDTH:3 * A_WIDTH + A_HEADS] + fb).reshape(bd, n_q, A_HEADS)
        oa_s = fox_sample_two_pass(page_table, q_s, k_s, v_s, lf_s.transpose(0, 2, 1), cache_kt, cache_vt,
                                   cache_lft, layer=l, group=16).reshape(ms, A_WIDTH)
        ub_s = rest_s[:, :, o_ub:o_ub + B_WIDTH]
        qkvc_s = rest_s[:, :, o_qkvc:o_qkvc + 3 * C_WIDTH]
        pool_full = jnp.concatenate([state_pool[l], ub_s], axis=1)
        conv_full = jnp.concatenate([state_conv[l], qkvc_s], axis=1)
        pre16 = jnp.concatenate([jnp.zeros((bd, 1, B_WIDTH), F32), state_pool[l]], axis=1)
        ob_s = pool_mix(ub_s, pre16, pw, psc, tt=n_q, n_valid=POOL_BUF, out_dtype=F32).reshape(ms, B_WIDTH)
        pad_t = lambda a: jnp.pad(a, ((0, 0), (0, 8 - n_q), (0, 0)))
        pre8 = jnp.concatenate([jnp.zeros((bd, 8 - CONV_BUF, 3 * C_WIDTH), F32), state_conv[l]], axis=1)
        oc_s, ssm_s = gdn_mixer(pad_t(qkvc_s), pad_t(rest_s[:, :, o_ab:o_ab + LANES]),
                                pad_t(rest_s[:, :, o_z:o_z + C_WIDTH]), pre8, state_ssm[l],
                                gdn_conv_w[l], a_log_row, dt_row, norm_w, chunk=8, n_chunks=1, t_valid=n_q,
                                out_dtype=F32)
        ys = merge_ln(hs, oa_s, ob_s, oc_s[:, :n_q].reshape(ms, C_WIDTH), *w_merge, lg[1], lb[1], tm=tm_s)
        ys = ffn_ln(ys, ffn_up[1], ffn_dn[1], lg[2], lb[2], tm=tm_s)
        for lst, a in zip(outs_s, (k_s.reshape(bd, n_q, A_HEADS, A_HEAD_DIM), v_s.reshape(bd, n_q, A_HEADS, A_HEAD_DIM),
                                   lf_s, pool_full[:, -POOL_BUF:], conv_full[:, -CONV_BUF:], ssm_s)):
            lst.append(a)

    k_p, v_p, lf_p, pool_p, conv_p, ssm_p = [jnp.stack(a) for a in outs_p]
    k_s, v_s, lf_s, pool_s, conv_s, ssm_s = [jnp.stack(a) for a in outs_s]
    return (yp.reshape(bp, seq, D_MODEL), ys.reshape(bd, n_q, D_MODEL), k_p, v_p, lf_p, pool_p, conv_p, ssm_p,
            k_s, v_s, lf_s, pool_s, conv_s, ssm_s)
```

```python
import functools

import jax
import jax.numpy as jnp
from jax import lax
from jax.experimental import pallas as pl
from jax.experimental.pallas import tpu as pltpu

F32 = jnp.float32
BF16 = jnp.bfloat16
HIGHEST = lax.Precision.HIGHEST

D_MODEL = 1024
DEPTH = 2
PAGE_SIZE = 128
A_HEADS = 16
A_HEAD_DIM = 64
A_WIDTH = A_HEADS * A_HEAD_DIM
POOL_WINDOWS = (2, 4, 8, 16)
POOL_GROUP_DIM = 128
B_WIDTH = len(POOL_WINDOWS) * POOL_GROUP_DIM
POOL_BUF = max(POOL_WINDOWS) - 1
C_HEADS = 4
C_HEAD_DIM = 128
C_WIDTH = C_HEADS * C_HEAD_DIM
CONV_WIDTH = 4
CONV_BUF = CONV_WIDTH - 1
N_BRANCH = 3
D_FF = ((8 * D_MODEL // 3 + 127) // 128) * 128
DN_ALPHA = (2 * DEPTH) ** 0.25
LN_EPS = 1e-5
RMS_EPS = 1e-6
L2_EPS = 1e-6
NEG_BIG = -1e30
LOG2E = 1.4426950408889634

VMEM_LIMIT_BYTES = 56 * 1024 * 1024
LANES = 128


def _params(*sem):
    return pltpu.CompilerParams(dimension_semantics=sem, vmem_limit_bytes=VMEM_LIMIT_BYTES)


def _layer_norm(y, g, b):
    mu = jnp.mean(y, axis=-1, keepdims=True)
    d = y - mu
    var = jnp.mean(d * d, axis=-1, keepdims=True)
    return d * lax.rsqrt(var + LN_EPS) * g + b


def _log_sigmoid(x):
    return jnp.minimum(x, 0.0) - jnp.log(1.0 + jnp.exp(-jnp.abs(x)))


def _softplus(x):
    return jnp.maximum(x, 0.0) + jnp.log(1.0 + jnp.exp(-jnp.abs(x)))


def _sigmoid(x):
    return 1.0 / (1.0 + jnp.exp(-x))


def _silu(x):
    return x * _sigmoid(x)


def _dot(a, b):
    return jnp.dot(a, b, preferred_element_type=F32)


def _dot_nt(a, b, precision=None):
    return lax.dot_general(a, b, (((1,), (1,)), ((), ())), preferred_element_type=F32, precision=precision)


def _dot_hi(a, b):
    return jnp.dot(a, b, preferred_element_type=F32, precision=HIGHEST)


def _split(a):
    hi = a.astype(BF16)
    return hi, (a - hi.astype(F32)).astype(BF16)


def _mm_split(a, b):
    (ah, al), (bh, bl) = a, b
    return _dot(ah, bh) + (_dot(ah, bl) + _dot(al, bh))


FFN_COLUMN_SPLITS = (0, 768, 1536, 2304, D_FF)


def _ffn_ln_kernel(x_ref, wup_ref, wd_ref, g_ref, b_ref, o_ref):
    x = x_ref[...]
    xb = x.astype(BF16)
    groups = list(zip(FFN_COLUMN_SPLITS[:-1], FFN_COLUMN_SPLITS[1:]))
    gate_up = lambda lo, hi: (_dot(xb, wup_ref[:, lo:hi]), _dot(xb, wup_ref[:, D_FF + lo:D_FF + hi]))
    pending = gate_up(*groups[0])
    acc = None
    for gi, (lo, hi) in enumerate(groups):
        gate, up = pending
        if gi + 1 < len(groups):
            pending = gate_up(*groups[gi + 1])
        part = _dot((_silu(gate) * up).astype(BF16), wd_ref[lo:hi, :])
        acc = part if acc is None else acc + part
    o_ref[...] = _layer_norm(DN_ALPHA * x + 0.5 * acc, g_ref[...], b_ref[...])


def ffn_ln(x, w_up, w_down, g, b, *, tm):
    m = x.shape[0]
    const = lambda i: (0, 0)
    resident = dict(pipeline_mode=pl.Buffered(1))
    return pl.pallas_call(
        _ffn_ln_kernel,
        grid=(m // tm,),
        in_specs=[
            pl.BlockSpec((tm, D_MODEL), lambda i: (i, 0)),
            pl.BlockSpec((D_MODEL, 2 * D_FF), const, **resident),
            pl.BlockSpec((D_FF, D_MODEL), const, **resident),
            pl.BlockSpec((1, D_MODEL), const),
            pl.BlockSpec((1, D_MODEL), const),
        ],
        out_specs=pl.BlockSpec((tm, D_MODEL), lambda i: (i, 0)),
        out_shape=jax.ShapeDtypeStruct((m, D_MODEL), F32),
        compiler_params=_params("parallel"),
        name="ffn_ln",
    )(x, w_up, w_down, g, b)


def _inproj_prompt_kernel(h_ref, wqt_ref, wk_ref, wkt_ref, wvt_ref, wft_ref, fb_ref, wc_ref, wu_ref, wz_ref, wab_ref,
                          qt_ref, kb_ref, kt_ref, vt_ref, vtb_ref, lft_ref, qn_ref, kn_ref, c_ref, u_ref, z_ref, ab_ref):
    hb = h_ref[...].astype(BF16)
    tm = hb.shape[0]
    head_sq = lambda a: jnp.sum((a * a).reshape(A_HEADS, A_HEAD_DIM, tm), axis=1)
    qt = _dot_nt(wqt_ref[...], hb).astype(BF16)
    qt_ref[0] = qt
    qn_ref[0] = head_sq(qt.astype(F32))
    kb_ref[0] = _dot(hb, wk_ref[...]).astype(BF16)
    kt = _dot_nt(wkt_ref[...], hb)
    kt_ref[0] = kt
    kn_ref[0] = head_sq(kt)
    vt = _dot_nt(wvt_ref[...], hb)
    vt_ref[0] = vt
    vtb_ref[0] = vt.astype(BF16)
    lft_ref[0] = _log_sigmoid(_dot_nt(wft_ref[...], hb) + fb_ref[...])
    c_ref[0] = _dot(hb, wc_ref[...])
    u_ref[0] = _dot(hb, wu_ref[...])
    z_ref[0] = _dot(hb, wz_ref[...])
    ab_ref[0] = _dot(hb, wab_ref[...])


def inproj_prompt(h, wqt, wk, wkt, wvt, wft, fbias, wc, wu, wz, wab, *, batch, seq, tm):
    m = h.shape[0]
    nt = seq // tm
    const = lambda i: (0, 0)
    tok3 = lambda i: (i // nt, i % nt, 0)
    feat3 = lambda i: (i // nt, 0, i % nt)
    weights = (wqt, wk, wkt, wvt, wft, fbias, wc, wu, wz, wab)
    tok_outs = ((A_WIDTH, BF16), (3 * C_WIDTH, F32), (B_WIDTH, F32), (C_WIDTH, F32), (LANES, F32))
    feat_outs = ((A_WIDTH, BF16), (A_WIDTH, F32), (A_WIDTH, F32), (A_WIDTH, BF16), (A_HEADS, F32),
                 (A_HEADS, F32), (A_HEADS, F32))
    specs = ([(feat_outs[0], True), (tok_outs[0], False)] + [(f, True) for f in feat_outs[1:]]
             + [(t, False) for t in tok_outs[1:]])
    return pl.pallas_call(
        _inproj_prompt_kernel,
        grid=(m // tm,),
        in_specs=[pl.BlockSpec((tm, D_MODEL), lambda i: (i, 0))] + [pl.BlockSpec(w.shape, const) for w in weights],
        out_specs=[pl.BlockSpec((1, n, tm), feat3) if feat else pl.BlockSpec((1, tm, n), tok3)
                   for (n, _), feat in specs],
        out_shape=[jax.ShapeDtypeStruct((batch, n, seq) if feat else (batch, seq, n), dt)
                   for (n, dt), feat in specs],
        compiler_params=_params("parallel"),
        name="inproj_prompt",
    )(h, *weights)


def _matmul_kernel(x_ref, w_ref, o_ref):
    o_ref[...] = _dot(x_ref[...].astype(BF16), w_ref[...])


def matmul_rows(x, w, *, tm):
    m, k = x.shape
    n = w.shape[1]
    return pl.pallas_call(
        _matmul_kernel,
        grid=(m // tm,),
        in_specs=[pl.BlockSpec((tm, k), lambda i: (i, 0)), pl.BlockSpec((k, n), lambda i: (0, 0))],
        out_specs=pl.BlockSpec((tm, n), lambda i: (i, 0)),
        out_shape=jax.ShapeDtypeStruct((m, n), F32),
        compiler_params=_params("parallel"),
        name="matmul_rows",
    )(x, w)


def _cumsum_lanes_kernel(x_ref, o_ref, r_ref, carry_ref, *, tb):
    @pl.when(pl.program_id(1) == 0)
    def _():
        carry_ref[...] = jnp.zeros_like(carry_ref)

    h = x_ref.shape[1]
    r = lax.broadcasted_iota(jnp.int32, (tb, tb), 0)
    c = lax.broadcasted_iota(jnp.int32, (tb, tb), 1)
    upper = (r <= c).astype(F32)
    out = _dot_hi(x_ref[0], upper) + carry_ref[...]
    carry_ref[...] = out[:, tb - 1:tb]
    r_ref[0] = out * LOG2E
    cols = jnp.concatenate([out * LOG2E, jnp.zeros((LANES - h, tb), F32)], axis=0).T
    for i in range(h):
        o_ref[0, i] = cols[:, i:i + 1]


def cumsum_lanes(x, *, tb):
    b, h, t = x.shape
    return pl.pallas_call(
        functools.partial(_cumsum_lanes_kernel, tb=tb),
        grid=(b, t // tb),
        in_specs=[pl.BlockSpec((1, h, tb), lambda i, j: (i, 0, j))],
        out_specs=[pl.BlockSpec((1, h, tb, 1), lambda i, j: (i, 0, j, 0)),
                   pl.BlockSpec((1, h, tb), lambda i, j: (i, 0, j))],
        out_shape=[jax.ShapeDtypeStruct((b, h, t, 1), F32), jax.ShapeDtypeStruct((b, h, t), F32)],
        scratch_shapes=[pltpu.VMEM((h, 1), F32)],
        compiler_params=_params("parallel", "arbitrary"),
        name="cumsum_lanes",
    )(x)


def _fox_prompt_kernel(fast_ref, qt_ref, k_ref, vt_ref, c_ref, cr_ref, qn_ref, kn_ref, o_ref, km_ref, *, tq):
    i = pl.program_id(2)
    plan = fast_ref[(pl.program_id(0) * pl.num_programs(1) + pl.program_id(1)) * pl.num_programs(2) + i]
    fast = plan & 1
    first = plan >> 1

    @pl.when(i == 0)
    def _():
        kb = k_ref[0]
        low = lax.broadcasted_iota(jnp.int32, kb.shape, 1) < A_HEAD_DIM
        zero = jnp.zeros_like(kb)
        km_ref[0] = jnp.where(low, kb, zero)
        km_ref[1] = jnp.where(low, zero, kb)

    qt = qt_ref[0]
    key = lax.broadcasted_iota(jnp.int32, (tq, tq), 0)
    qry = lax.broadcasted_iota(jnp.int32, (tq, tq), 1)
    causal = key <= qry

    def scores_of(j, masked):
        off = pl.multiple_of(j * tq, tq)
        out = []
        for hh in range(2):
            s = _dot(km_ref[hh, pl.ds(off, tq), :], qt) - c_ref[0, hh, pl.ds(off, tq), :]
            out.append(jnp.where(causal, s, NEG_BIG) if masked else s)
        return out, [vt_ref[0, hh * A_HEAD_DIM:(hh + 1) * A_HEAD_DIM, pl.ds(off, tq)] for hh in range(2)]

    def finish(a0, l0, a1, l1):
        o_ref[0] = jnp.concatenate([a0 / l0, a1 / l1], axis=0).T.astype(o_ref.dtype)

    @pl.when(fast != 0)
    def _():
        ref = [qn_ref[0, 0, hh:hh + 1, :] * kn_ref[0, 0, hh:hh + 1, 0:1] - cr_ref[0, 0, hh:hh + 1, :]
               for hh in range(2)]

        def step(j, carry, masked):
            scores, vt = scores_of(j, masked)
            new = []
            for hh in range(2):
                l, acc = carry[hh]
                p = jnp.exp2(scores[hh] - ref[hh])
                new.append((l + jnp.sum(p, axis=0, keepdims=True), acc + _dot(vt[hh], p.astype(BF16))))
            return tuple(new)

        one = (jnp.zeros((1, tq), F32), jnp.zeros((A_HEAD_DIM, tq), F32))
        carry = lax.fori_loop(first, i, lambda j, c: step(j, c, False), (one, one))
        hq = tq // 2
        lo_off = pl.multiple_of(i * tq, tq)
        hi_off = pl.multiple_of(i * tq + hq, hq)
        tri_lo = lax.broadcasted_iota(jnp.int32, (hq, tq), 0) <= lax.broadcasted_iota(jnp.int32, (hq, tq), 1)
        tri_hi = lax.broadcasted_iota(jnp.int32, (hq, hq), 0) <= lax.broadcasted_iota(jnp.int32, (hq, hq), 1)
        done = []
        for hh in range(2):
            l, acc = carry[hh]
            rows_v = slice(hh * A_HEAD_DIM, (hh + 1) * A_HEAD_DIM)
            s_lo = _dot(km_ref[hh, pl.ds(lo_off, hq), :], qt) - c_ref[0, hh, pl.ds(lo_off, hq), :]
            s_hi = _dot(km_ref[hh, pl.ds(hi_off, hq), :], qt[:, hq:]) - c_ref[0, hh, pl.ds(hi_off, hq), :]
            p_lo = jnp.exp2(jnp.where(tri_lo, s_lo, NEG_BIG) - ref[hh])
            p_hi = jnp.exp2(jnp.where(tri_hi, s_hi, NEG_BIG) - ref[hh][:, hq:])
            l_hi = jnp.sum(p_hi, axis=0, keepdims=True)
            a_hi = _dot(vt_ref[0, rows_v, pl.ds(hi_off, hq)], p_hi.astype(BF16))
            l = l + jnp.sum(p_lo, axis=0, keepdims=True) + jnp.concatenate([jnp.zeros_like(l_hi), l_hi], axis=1)
            acc = (acc + _dot(vt_ref[0, rows_v, pl.ds(lo_off, hq)], p_lo.astype(BF16))
                   + jnp.concatenate([jnp.zeros_like(a_hi), a_hi], axis=1))
            done.append((l, acc))
        (l0, a0), (l1, a1) = done
        finish(a0, l0, a1, l1)

    @pl.when(fast == 0)
    def _():
        def step(j, carry, masked):
            scores, vt = scores_of(j, masked)
            new = []
            for hh in range(2):
                m, l, acc = carry[hh]
                m_new = jnp.maximum(m, jnp.max(scores[hh], axis=0, keepdims=True))
                alpha = jnp.exp2(m - m_new)
                p = jnp.exp2(scores[hh] - m_new)
                new.append((m_new, alpha * l + jnp.sum(p, axis=0, keepdims=True),
                            alpha * acc + _dot(vt[hh], p.astype(BF16))))
            return tuple(new)

        one = (jnp.full((1, tq), NEG_BIG, F32), jnp.zeros((1, tq), F32), jnp.zeros((A_HEAD_DIM, tq), F32))
        carry = lax.fori_loop(0, i, lambda j, c: step(j, c, False), (one, one))
        (_, l0, a0), (_, l1, a1) = step(i, carry, True)
        finish(a0, l0, a1, l1)


FOX_FAST_BOUND = 50.0
FOX_ZERO_EXP2 = -152.0


def fox_prompt(qt, k, vt, c_col, c_row, qn2, kn2, *, tq):
    b, t, _ = k.shape
    pairs = A_HEADS // 2
    nq = t // tq
    qn = jnp.sqrt(qn2)
    kn = jnp.sqrt(jnp.max(kn2, axis=-1)) * 1.01 + 1e-6
    fast = jnp.max((qn * kn[..., None]).reshape(b, pairs, 2, nq, tq), axis=(2, 4)) <= FOX_FAST_BOUND
    c_blocks = c_row.reshape(b, pairs, 2, nq, tq)
    gap = c_blocks[..., None, :, tq - 1] - c_blocks[..., :, None, 0]
    dead = jnp.all(gap >= -FOX_ZERO_EXP2, axis=2) & (jnp.arange(nq)[None, :] < jnp.arange(nq)[:, None])
    first = jnp.sum(jnp.cumprod(dead.astype(jnp.int32), axis=-1), axis=-1)
    plan = fast.astype(jnp.int32) + 2 * jnp.where(fast, first, 0)
    kn_l = jnp.broadcast_to(kn.reshape(b, pairs, 2, 1), (b, pairs, 2, LANES))
    pair_row = lambda bi, hp, i, f: (bi, hp, 0, i)
    grid_spec = pltpu.PrefetchScalarGridSpec(
        num_scalar_prefetch=1,
        grid=(b, pairs, nq),
        in_specs=[
            pl.BlockSpec((1, LANES, tq), lambda bi, hp, i, f: (bi, hp, i)),
            pl.BlockSpec((1, t, LANES), lambda bi, hp, i, f: (bi, 0, hp)),
            pl.BlockSpec((1, LANES, t), lambda bi, hp, i, f: (bi, hp, 0)),
            pl.BlockSpec((1, 2, t, 1), lambda bi, hp, i, f: (bi, hp, 0, 0)),
            pl.BlockSpec((1, 1, 2, tq), pair_row),
            pl.BlockSpec((1, 1, 2, tq), pair_row),
            pl.BlockSpec((1, 1, 2, LANES), lambda bi, hp, i, f: (bi, hp, 0, 0)),
        ],
        out_specs=pl.BlockSpec((1, tq, LANES), lambda bi, hp, i, f: (bi, i, hp)),
        scratch_shapes=[pltpu.VMEM((2, t, LANES), BF16)],
    )
    return pl.pallas_call(
        functools.partial(_fox_prompt_kernel, tq=tq),
        grid_spec=grid_spec,
        out_shape=jax.ShapeDtypeStruct((b, t, A_WIDTH), BF16),
        compiler_params=_params("parallel", "parallel", "arbitrary"),
        name="fox_prompt",
    )(plan.reshape(-1), qt, k, vt, c_col, c_row.reshape(b, pairs, 2, t),
      qn.reshape(b, pairs, 2, t), kn_l)


def _head_mask():
    hrow = lax.broadcasted_iota(jnp.int32, (A_HEADS, A_WIDTH), 0)
    hcol = lax.broadcasted_iota(jnp.int32, (A_HEADS, A_WIDTH), 1) // A_HEAD_DIM
    return hrow == hcol


def _block_diag_queries(q_ref, n_q):
    mask = _head_mask()
    return jnp.concatenate(
        [jnp.where(mask, jnp.broadcast_to(q_ref[0, qi:qi + 1, :], (A_HEADS, A_WIDTH)), 0.0) for qi in range(n_q)],
        axis=0)


def _fox_scores_kernel(pt_ref, q_ref, *rest, n_q, group):
    kp_refs, lfp_refs = rest[:group], rest[group:2 * group]
    s_ref, qbd_ref, carry_ref = rest[2 * group:]

    @pl.when(pl.program_id(1) == 0)
    def _():
        carry_ref[...] = jnp.zeros_like(carry_ref)
        qbd_ref[...] = _block_diag_queries(q_ref, n_q).astype(BF16)

    qbd = qbd_ref[...]
    r = lax.broadcasted_iota(jnp.int32, (PAGE_SIZE, PAGE_SIZE), 0)
    c = lax.broadcasted_iota(jnp.int32, (PAGE_SIZE, PAGE_SIZE), 1)
    after = (r > c).astype(F32)
    carry = carry_ref[...]
    for g in range(group):
        lf = lfp_refs[g][...]
        bias = _dot_hi(lf, after) + carry
        carry = carry + jnp.sum(lf, axis=-1, keepdims=True)
        s = _dot(qbd, kp_refs[g][...].astype(BF16)) + jnp.concatenate([bias] * n_q, axis=0)
        s_ref[0, :, (group - 1 - g) * PAGE_SIZE:(group - g) * PAGE_SIZE] = s
    carry_ref[...] = carry


def _fox_values_kernel(vpage_ref, pos_ref, nlive_ref, q_ref, kn_ref, vn_ref, lfn_ref, s_ref, mpast_ref, *rest,
                       n_q, group, n_pages):
    vp_refs = rest[:group]
    o_ref, m_ref, l_ref, acc_ref = rest[group:]
    b = pl.program_id(0)
    p = pl.program_id(1)
    rows = n_q * A_HEADS

    @pl.when(p == 0)
    def _():
        qbd = _block_diag_queries(q_ref, n_q)
        lfn = lfn_ref[0]
        qidx = lax.broadcasted_iota(jnp.int32, (rows, 1), 0) // A_HEADS
        s_new = []
        cum = jnp.zeros((A_HEADS, 1), F32)
        for ki in range(n_q):
            cum = cum + lfn[:, ki:ki + 1]
            sk = jnp.sum(qbd * kn_ref[0, ki:ki + 1, :], axis=-1, keepdims=True)
            sk = sk - jnp.concatenate([cum] * n_q, axis=0)
            s_new.append(jnp.where(qidx >= ki, sk, NEG_BIG))
        m = mpast_ref[0]
        for sk in s_new:
            m = jnp.maximum(m, sk)
        l = jnp.zeros((rows, 1), F32)
        acc = jnp.zeros((rows, A_WIDTH), F32)
        for ki, sk in enumerate(s_new):
            pk = jnp.exp(sk - m)
            l = l + pk
            acc = acc + pk * vn_ref[0, ki:ki + 1, :]
        m_ref[...] = m
        l_ref[...] = l
        acc_ref[...] = acc

    @pl.when(p * group < nlive_ref[b])
    def _():
        m = m_ref[...]
        l = l_ref[...]
        pv = None
        for g in range(group):
            pos = pos_ref[b * n_pages + p * group + g]
            s = s_ref[0, :, pl.ds(pl.multiple_of(pos * PAGE_SIZE, PAGE_SIZE), PAGE_SIZE)]
            pr = jnp.exp(s - m)
            l = l + jnp.sum(pr, axis=-1, keepdims=True)
            t = _dot_nt(pr.astype(BF16), vp_refs[g][...].astype(BF16))
            pv = t if pv is None else pv + t
        l_ref[...] = l
        acc_ref[...] += pv

    @pl.when(p == pl.num_programs(1) - 1)
    def _():
        mask = _head_mask()
        acc = acc_ref[...] / l_ref[...]
        for qi in range(n_q):
            blk = jnp.where(mask, acc[qi * A_HEADS:(qi + 1) * A_HEADS, :], 0.0)
            o_ref[0, qi:qi + 1, :] = jnp.sum(blk, axis=0, keepdims=True)


FOX_ZERO_EXP = -106.0


def fox_sample_two_pass(page_table, q, k_new, v_new, lf_new_t, cache_kt, cache_vt, cache_lft, *, layer, group,
                        value_group):
    bd, n_q, _ = q.shape
    n_pages = page_table.shape[1]
    rows = n_q * A_HEADS
    n_groups = n_pages // group
    assert n_pages % group == 0
    seq3 = lambda b, p, *_: (b, 0, 0)

    def page(g):
        return lambda b, p, pt_ref: (layer, pt_ref[b * n_pages + (n_pages - 1 - (p * group + g))], 0, 0)

    scores = pl.pallas_call(
        functools.partial(_fox_scores_kernel, n_q=n_q, group=group),
        grid_spec=pltpu.PrefetchScalarGridSpec(
            num_scalar_prefetch=1,
            grid=(bd, n_groups),
            in_specs=[pl.BlockSpec((1, n_q, A_WIDTH), seq3)]
            + [pl.BlockSpec((None, None, A_WIDTH, PAGE_SIZE), page(g)) for g in range(group)]
            + [pl.BlockSpec((None, None, A_HEADS, PAGE_SIZE), page(g)) for g in range(group)],
            out_specs=pl.BlockSpec((1, rows, group * PAGE_SIZE), lambda b, p, pt_ref: (b, 0, n_groups - 1 - p)),
            scratch_shapes=[pltpu.VMEM((rows, A_WIDTH), BF16), pltpu.VMEM((A_HEADS, 1), F32)],
        ),
        out_shape=jax.ShapeDtypeStruct((bd, rows, n_pages * PAGE_SIZE), F32),
        compiler_params=_params("parallel", "arbitrary"),
        name="fox_sample_scores",
    )(page_table.reshape(-1), q, *([cache_kt] * group + [cache_lft] * group))

    page_max = jnp.max(scores.reshape(bd, rows, n_pages, PAGE_SIZE), axis=-1)
    m_past = jnp.max(page_max, axis=-1, keepdims=True)
    live = jnp.any(page_max - m_past > FOX_ZERO_EXP, axis=1)
    n_live = jnp.sum(live, axis=-1).astype(jnp.int32)
    newest_first = jnp.arange(n_pages - 1, -1, -1, dtype=jnp.int32)
    rank = jnp.where(live, 0, n_pages) + newest_first[None, :]
    pos = jnp.argsort(rank, axis=-1).astype(jnp.int32)
    slot = jnp.minimum(jnp.arange(n_pages, dtype=jnp.int32)[None, :], n_live[:, None] - 1)
    pos = jnp.take_along_axis(pos, slot, axis=-1)
    vpage = jnp.take_along_axis(page_table, pos, axis=-1)

    group = value_group
    assert n_pages % group == 0

    def vpage_map(g):
        return lambda b, p, vp_ref, pos_ref, nl_ref: (layer, vp_ref[b * n_pages + p * group + g], 0, 0)

    return pl.pallas_call(
        functools.partial(_fox_values_kernel, n_q=n_q, group=group, n_pages=n_pages),
        grid_spec=pltpu.PrefetchScalarGridSpec(
            num_scalar_prefetch=3,
            grid=(bd, n_pages // group),
            in_specs=[
                pl.BlockSpec((1, n_q, A_WIDTH), seq3),
                pl.BlockSpec((1, n_q, A_WIDTH), seq3),
                pl.BlockSpec((1, n_q, A_WIDTH), seq3),
                pl.BlockSpec((1, A_HEADS, n_q), seq3),
                pl.BlockSpec((1, rows, n_pages * PAGE_SIZE), seq3),
                pl.BlockSpec((1, rows, 1), seq3),
            ] + [pl.BlockSpec((None, None, A_WIDTH, PAGE_SIZE), vpage_map(g)) for g in range(group)],
            out_specs=pl.BlockSpec((1, n_q, A_WIDTH), seq3),
            scratch_shapes=[pltpu.VMEM((rows, 1), F32), pltpu.VMEM((rows, 1), F32), pltpu.VMEM((rows, A_WIDTH), F32)],
        ),
        out_shape=jax.ShapeDtypeStruct((bd, n_q, A_WIDTH), F32),
        compiler_params=_params("parallel", "arbitrary"),
        name="fox_sample_values",
    )(vpage.reshape(-1), pos.reshape(-1), n_live, q, k_new, v_new, lf_new_t, scores, m_past,
      *([cache_vt] * group))


def _pool_kernel(u_ref, pre_ref, w_ref, sc_ref, o_ref, ext_ref, *, tt, n_valid):
    halo = POOL_BUF + 1
    j = pl.program_id(1)

    @pl.when(j == 0)
    def _():
        ext_ref[0:halo, :] = pre_ref[0]

    @pl.when(j > 0)
    def _():
        ext_ref[0:halo, :] = ext_ref[tt:tt + halo, :]

    u = u_ref[0]
    ext_ref[halo:halo + tt, :] = u
    pos = j * tt + lax.broadcasted_iota(jnp.int32, (tt, 1), 0) + (1 + n_valid)
    outs = []
    for gi, w in enumerate(POOL_WINDOWS):
        sl = slice(gi * POOL_GROUP_DIM, (gi + 1) * POOL_GROUP_DIM)
        tot = u[:, sl]
        for k in range(1, w):
            tot = tot + ext_ref[halo - k:halo - k + tt, sl]
        cnt = jnp.minimum(pos, w).astype(F32)
        d = tot / cnt - u[:, sl]
        outs.append(_dot(d.astype(BF16), w_ref[gi]))
    o_ref[0] = (jnp.concatenate(outs, axis=-1) * sc_ref[...]).astype(o_ref.dtype)


def pool_mix(u, prefix16, pool_w, pool_scale, *, tt, n_valid, out_dtype):
    b, t, _ = u.shape
    halo = POOL_BUF + 1
    return pl.pallas_call(
        functools.partial(_pool_kernel, tt=tt, n_valid=n_valid),
        grid=(b, t // tt),
        in_specs=[
            pl.BlockSpec((1, tt, B_WIDTH), lambda i, j: (i, j, 0)),
            pl.BlockSpec((1, halo, B_WIDTH), lambda i, j: (i, 0, 0)),
            pl.BlockSpec((len(POOL_WINDOWS), POOL_GROUP_DIM, POOL_GROUP_DIM), lambda i, j: (0, 0, 0)),
            pl.BlockSpec((1, B_WIDTH), lambda i, j: (0, 0)),
        ],
        out_specs=pl.BlockSpec((1, tt, B_WIDTH), lambda i, j: (i, j, 0)),
        out_shape=jax.ShapeDtypeStruct((b, t, B_WIDTH), out_dtype),
        scratch_shapes=[pltpu.VMEM((tt + 2 * halo, B_WIDTH), F32)],
        compiler_params=_params("parallel", "arbitrary"),
        name="pool_mix",
    )(u, prefix16, pool_w, pool_scale)


def _gdn_kernel(x_ref, ab_ref, z_ref, pre_ref, s0_ref, cw_ref, alog_ref, dtb_ref, nw_ref,
                o_ref, s_out_ref, ext_ref, s_ref, *, chunk, n_chunks, t_valid):
    j = pl.program_id(1)
    halo = 8

    @pl.when(j == 0)
    def _():
        ext_ref[0:halo, :] = pre_ref[0]
        s_ref[...] = s0_ref[0]

    span = chunk * n_chunks

    @pl.when(j > 0)
    def _():
        ext_ref[0:halo, :] = ext_ref[span:span + halo, :]

    ext_ref[halo:halo + span, :] = x_ref[0]
    y_all = ext_ref[halo:halo + span, :] * cw_ref[CONV_WIDTH - 1:CONV_WIDTH, :]
    for k in range(1, CONV_WIDTH):
        y_all = y_all + ext_ref[halo - k:halo - k + span, :] * cw_ref[CONV_WIDTH - 1 - k:CONV_WIDTH - k, :]
    y_all = _silu(y_all)

    ab = ab_ref[0]
    valid = (j * span + lax.broadcasted_iota(jnp.int32, (span, 1), 0)) < t_valid
    g_all = jnp.where(valid, -jnp.exp(alog_ref[...]) * _softplus(ab + dtb_ref[...]), 0.0)
    beta_all = jnp.where(valid, _sigmoid(ab), 0.0)
    z_all = z_ref[0]
    r = lax.broadcasted_iota(jnp.int32, (chunk, chunk), 0)
    c = lax.broadcasted_iota(jnp.int32, (chunk, chunk), 1)
    tri = (r >= c).astype(F32)

    n = C_HEADS * chunk
    rows = [slice(h * chunk, (h + 1) * chunk) for h in range(C_HEADS)]
    row_head = lax.broadcasted_iota(jnp.int32, (n, LANES), 0) // chunk
    lane = lax.broadcasted_iota(jnp.int32, (n, LANES), 1)
    pick = lambda a, off: jnp.where(lane == row_head + off, jnp.concatenate([a] * C_HEADS, axis=0), 0.0)
    rr = lax.broadcasted_iota(jnp.int32, (n, n), 0)
    cc = lax.broadcasted_iota(jnp.int32, (n, n), 1)
    same = (rr // chunk) == (cc // chunk)

    def setup(ci):
        tok = slice(ci * chunk, (ci + 1) * chunk)
        y = y_all[tok]
        gcum = _dot_hi(tri, g_all[tok])
        heads = lambda off: jnp.concatenate(
            [y[:, off + h * C_HEAD_DIM:off + (h + 1) * C_HEAD_DIM] for h in range(C_HEADS)], axis=0)
        qs, ks, vs = heads(0), heads(C_WIDTH), heads(2 * C_WIDTH)
        qs = qs * lax.rsqrt(jnp.sum(qs * qs, axis=-1, keepdims=True) + L2_EPS) * (C_HEAD_DIM ** -0.5)
        ks = ks * lax.rsqrt(jnp.sum(ks * ks, axis=-1, keepdims=True) + L2_EPS)
        gd = pick(gcum, 0)
        gc = jnp.sum(gd, axis=-1, keepdims=True)
        gr = _dot_nt(jnp.ones((8, LANES), F32), gd, precision=HIGHEST)[0:1, :]
        beta = jnp.sum(pick(beta_all[tok], C_HEADS), axis=-1, keepdims=True)
        g_last = jnp.sum(pick(jnp.broadcast_to(gcum[chunk - 1:chunk, :], (chunk, LANES)), 0),
                         axis=-1, keepdims=True)
        gamma = jnp.exp(jnp.where(same & (rr >= cc), gc - gr, -jnp.inf))
        eg = jnp.exp(gc)
        ksb = ks.astype(BF16)
        return dict(
            nmat=jnp.where(same & (rr > cc), beta * _dot_nt(ksb, ksb) * gamma, 0.0),
            bv=_split(beta * vs), bk=_split(beta * ks * eg),
            qk=(_dot_nt(qs.astype(BF16), ksb) * gamma).astype(BF16),
            k_dec=ks * jnp.exp(g_last - gc), q_dec=qs * eg,
            decay=[jnp.exp(gcum[chunk - 1:chunk, h:h + 1]) for h in range(C_HEADS)],
            z=jnp.concatenate([z_all[tok, h * C_HEAD_DIM:(h + 1) * C_HEAD_DIM] for h in range(C_HEADS)], axis=0))

    parts = [setup(ci) for ci in range(n_chunks)]
    inv = [(rr == cc).astype(F32) - p["nmat"] for p in parts]
    pw = [_split(p["nmat"]) for p in parts]
    for _ in range(max(1, (chunk - 1).bit_length()) - 1):
        pw = [_split(_dot(x[0], x[0])) for x in pw]
        inv = [a + _mm_split(_split(a), x) for a, x in zip(inv, pw)]
    inv = [_split(a) for a in inv]
    for p, a in zip(parts, inv):
        p["u"] = _mm_split(a, p["bv"])
        p["w"] = _mm_split(a, p["bk"])
    state = [s_ref[h] for h in range(C_HEADS)]
    for ci, p in enumerate(parts):
        state_b = [s.astype(BF16) for s in state]
        v_new = jnp.concatenate(
            [p["u"][rows[h]] - _dot(p["w"][rows[h]].astype(BF16), state_b[h]) for h in range(C_HEADS)], axis=0)
        o = jnp.concatenate([_dot(p["q_dec"][rows[h]].astype(BF16), state_b[h]) for h in range(C_HEADS)], axis=0)
        o = o + _dot(p["qk"], v_new.astype(BF16))
        state = [state[h] * p["decay"][h] + lax.dot_general(
            p["k_dec"][rows[h]].astype(BF16), v_new[rows[h]].astype(BF16), (((0,), (0,)), ((), ())),
            preferred_element_type=F32) for h in range(C_HEADS)]
        o = o * lax.rsqrt(jnp.mean(o * o, axis=-1, keepdims=True) + RMS_EPS) * nw_ref[...]
        o = o * _silu(p["z"])
        o_ref[0, ci * chunk:(ci + 1) * chunk, :] = jnp.concatenate(
            [o[rows[h]] for h in range(C_HEADS)], axis=-1).astype(o_ref.dtype)
    for h in range(C_HEADS):
        s_ref[h] = state[h]

    @pl.when(j == pl.num_programs(1) - 1)
    def _():
        s_out_ref[0] = s_ref[...]


def gdn_mixer(x, ab, z, prefix8, s0, conv_w, a_log_row, dt_bias_row, norm_w, *, chunk, n_chunks, t_valid,
              out_dtype):
    b, t, _ = x.shape
    span = chunk * n_chunks
    row = lambda i, j: (0, 0)
    tok = lambda i, j: (i, j, 0)
    seq = lambda i, j: (i, 0, 0)
    return pl.pallas_call(
        functools.partial(_gdn_kernel, chunk=chunk, n_chunks=n_chunks, t_valid=t_valid),
        grid=(b, t // span),
        in_specs=[
            pl.BlockSpec((1, span, 3 * C_WIDTH), tok),
            pl.BlockSpec((1, span, LANES), tok),
            pl.BlockSpec((1, span, C_WIDTH), tok),
            pl.BlockSpec((1, 8, 3 * C_WIDTH), seq),
            pl.BlockSpec((1, C_HEADS, C_HEAD_DIM, C_HEAD_DIM), lambda i, j: (i, 0, 0, 0)),
            pl.BlockSpec((CONV_WIDTH, 3 * C_WIDTH), row),
            pl.BlockSpec((1, LANES), row),
            pl.BlockSpec((1, LANES), row),
            pl.BlockSpec((1, C_HEAD_DIM), row),
        ],
        out_specs=[
            pl.BlockSpec((1, span, C_WIDTH), tok),
            pl.BlockSpec((1, C_HEADS, C_HEAD_DIM, C_HEAD_DIM), lambda i, j: (i, 0, 0, 0)),
        ],
        out_shape=[
            jax.ShapeDtypeStruct((b, t, C_WIDTH), out_dtype),
            jax.ShapeDtypeStruct((b, C_HEADS, C_HEAD_DIM, C_HEAD_DIM), F32),
        ],
        scratch_shapes=[
            pltpu.VMEM((span + 16, 3 * C_WIDTH), F32),
            pltpu.VMEM((C_HEADS, C_HEAD_DIM, C_HEAD_DIM), F32),
        ],
        compiler_params=_params("parallel", "arbitrary"),
        name="gdn_mixer",
    )(x, ab, z, prefix8, s0, conv_w, a_log_row, dt_bias_row, norm_w)


def _merge_kernel(h_ref, oa_ref, ob_ref, oc_ref, wg_ref, wa_ref, wb_ref, wc_ref, wo_ref, g_ref, b_ref, o_ref):
    h = h_ref[...]
    hb = h.astype(BF16)
    merged = None
    for i, (x_ref, w_ref) in enumerate(((oa_ref, wa_ref), (ob_ref, wb_ref), (oc_ref, wc_ref))):
        gate = _sigmoid(_dot(hb, wg_ref[:, i * D_MODEL:(i + 1) * D_MODEL]))
        term = gate * _dot(x_ref[...].astype(BF16), w_ref[...])
        merged = term if merged is None else merged + term
    y = DN_ALPHA * h + _dot(merged.astype(BF16), wo_ref[...])
    o_ref[...] = _layer_norm(y, g_ref[...], b_ref[...])


def merge_ln(h, oa, ob, oc, wg, wa, wb, wc, wo, g, b, *, tm):
    m = h.shape[0]
    const = lambda i: (0, 0)
    rows = lambda i: (i, 0)
    return pl.pallas_call(
        _merge_kernel,
        grid=(m // tm,),
        in_specs=[
            pl.BlockSpec((tm, D_MODEL), rows),
            pl.BlockSpec((tm, A_WIDTH), rows),
            pl.BlockSpec((tm, B_WIDTH), rows),
            pl.BlockSpec((tm, C_WIDTH), rows),
            pl.BlockSpec((D_MODEL, N_BRANCH * D_MODEL), const),
            pl.BlockSpec((A_WIDTH, D_MODEL), const),
            pl.BlockSpec((B_WIDTH, D_MODEL), const),
            pl.BlockSpec((C_WIDTH, D_MODEL), const),
            pl.BlockSpec((D_MODEL, D_MODEL), const),
            pl.BlockSpec((1, D_MODEL), const),
            pl.BlockSpec((1, D_MODEL), const),
        ],
        out_specs=pl.BlockSpec((tm, D_MODEL), rows),
        out_shape=jax.ShapeDtypeStruct((m, D_MODEL), F32),
        compiler_params=_params("parallel"),
        name="merge_ln",
    )(h, oa, ob, oc, wg, wa, wb, wc, wo, g, b)


def _split_w_in(w_in_l):
    o = 0
    parts = {}
    for name, n in (("q", A_WIDTH), ("k", A_WIDTH), ("v", A_WIDTH), ("f", A_HEADS), ("ub", B_WIDTH),
                    ("qkvc", 3 * C_WIDTH), ("a", C_HEADS), ("b", C_HEADS), ("z", C_WIDTH),
                    ("gate", N_BRANCH * D_MODEL)):
        parts[name] = w_in_l[:, o:o + n]
        o += n
    return parts


def kernel(x_prompt, x_sample, cache_k, cache_v, cache_logf, page_table, state_pool, state_conv, state_ssm,
           w_in, fox_f_bias, gdn_conv_w, gdn_a_log, gdn_dt_bias, gdn_norm_w, pool_w, pool_scale,
           w_branch_a, w_branch_b, w_branch_c, w_out, ffn_w_in, ffn_w_out, ln_g, ln_b):
    bp, seq, _ = x_prompt.shape
    bd, n_q, _ = x_sample.shape
    mp, ms = bp * seq, bd * n_q
    tm_p = 512
    tm_s = ms

    n_pool = cache_k.shape[1]
    cache_kt = cache_k.transpose(0, 1, 3, 4, 2).reshape(DEPTH, n_pool, A_WIDTH, PAGE_SIZE)
    cache_vt = cache_v.transpose(0, 1, 3, 4, 2).reshape(DEPTH, n_pool, A_WIDTH, PAGE_SIZE)
    cache_lft = cache_logf.transpose(0, 1, 3, 2)

    yp = x_prompt.reshape(mp, D_MODEL)
    ys = x_sample.reshape(ms, D_MODEL)
    outs_p = ([], [], [], [], [], [])
    outs_s = ([], [], [], [], [], [])
    zeros_ab = jnp.zeros((D_MODEL, LANES - 2 * C_HEADS), F32)
    for l in range(DEPTH):
        wp = _split_w_in(w_in[l])
        wq = (wp["q"] * (A_HEAD_DIM ** -0.5)).astype(BF16)
        w_ab = jnp.concatenate([wp["a"], wp["b"], zeros_ab], axis=1)
        w_rest = jnp.concatenate([wp["qkvc"], wp["ub"], wp["z"], w_ab], axis=1).astype(BF16)
        o_qkvc, o_ub, o_z, o_ab = 0, 3 * C_WIDTH, 3 * C_WIDTH + B_WIDTH, 3 * C_WIDTH + B_WIDTH + C_WIDTH
        w_f_pad = jnp.concatenate([wp["f"], jnp.zeros((D_MODEL, LANES - A_HEADS), F32)], axis=1)
        w_sample = jnp.concatenate([wq.astype(F32), wp["k"], wp["v"], w_f_pad], axis=1).astype(BF16)
        w_gate = wp["gate"].astype(BF16)
        ffn_up = ffn_w_in[l].astype(BF16)
        ffn_dn = ffn_w_out[l].astype(BF16)
        lg = ln_g[l].reshape(3, 1, D_MODEL)
        lb = ln_b[l].reshape(3, 1, D_MODEL)
        fb = fox_f_bias[l]
        a_log_row = jnp.zeros((1, LANES), F32).at[0, :C_HEADS].set(gdn_a_log[l])
        dt_row = jnp.zeros((1, LANES), F32).at[0, :C_HEADS].set(gdn_dt_bias[l])
        norm_w = gdn_norm_w[l].reshape(1, C_HEAD_DIM)
        pw = pool_w[l].astype(BF16)
        psc = pool_scale[l].reshape(1, B_WIDTH)
        w_merge = (w_gate, w_branch_a[l].astype(BF16), w_branch_b[l].astype(BF16),
                   w_branch_c[l].astype(BF16), w_out[l].astype(BF16))

        hp = ffn_ln(yp, ffn_up[0], ffn_dn[0], lg[0], lb[0], tm=tm_p)
        hs = ffn_ln(ys, ffn_up[0], ffn_dn[0], lg[0], lb[0], tm=tm_s)

        wqt = (wp["q"] * (A_HEAD_DIM ** -0.5 * LOG2E)).T.astype(BF16)
        qt_p, kb_p, kt_p, vt_p, vtb_p, lft_p, qn2_p, kn2_p, qkvc_p, ub_p, z_p, ab_p = inproj_prompt(
            hp, wqt, wp["k"].astype(BF16), wp["k"].T.astype(BF16), wp["v"].T.astype(BF16), wp["f"].T.astype(BF16),
            fb.reshape(A_HEADS, 1), wp["qkvc"].astype(BF16), wp["ub"].astype(BF16), wp["z"].astype(BF16),
            w_ab.astype(BF16), batch=bp, seq=seq, tm=256)
        ccol_p, crow_p = cumsum_lanes(lft_p, tb=512)
        oa_p = fox_prompt(qt_p, kb_p, vtb_p, ccol_p, crow_p, qn2_p, kn2_p, tq=1024).reshape(mp, A_WIDTH)
        ob_p = pool_mix(ub_p, jnp.zeros((bp, POOL_BUF + 1, B_WIDTH), F32), pw, psc,
                        tt=512, n_valid=0, out_dtype=BF16).reshape(mp, B_WIDTH)
        oc_p, ssm_p = gdn_mixer(qkvc_p, ab_p, z_p,
                                jnp.zeros((bp, 8, 3 * C_WIDTH), F32),
                                jnp.zeros((bp, C_HEADS, C_HEAD_DIM, C_HEAD_DIM), F32),
                                gdn_conv_w[l], a_log_row, dt_row, norm_w, chunk=64, n_chunks=4, t_valid=seq,
                                out_dtype=BF16)
        yp = merge_ln(hp, oa_p, ob_p, oc_p.reshape(mp, C_WIDTH), *w_merge, lg[1], lb[1], tm=tm_p)
        yp = ffn_ln(yp, ffn_up[1], ffn_dn[1], lg[2], lb[2], tm=tm_p)
        k_p = kt_p.reshape(bp, A_HEADS, A_HEAD_DIM, seq).transpose(0, 3, 1, 2)
        v_p = vt_p.reshape(bp, A_HEADS, A_HEAD_DIM, seq).transpose(0, 3, 1, 2)
        for lst, a in zip(outs_p, (k_p, v_p, lft_p.transpose(0, 2, 1), ub_p[:, seq - POOL_BUF:],
                                   qkvc_p[:, seq - CONV_BUF:], ssm_p)):
            lst.append(a)

        proj_s = matmul_rows(hs, w_sample, tm=tm_s)
        rest_s = matmul_rows(hs, w_rest, tm=tm_s).reshape(bd, n_q, -1)
        q_s = proj_s[:, :A_WIDTH].reshape(bd, n_q, A_WIDTH)
        k_s = proj_s[:, A_WIDTH:2 * A_WIDTH].reshape(bd, n_q, A_WIDTH)
        v_s = proj_s[:, 2 * A_WIDTH:3 * A_WIDTH].reshape(bd, n_q, A_WIDTH)
        lf_s = jax.nn.log_sigmoid(proj_s[:, 3 * A_WIDTH:3 * A_WIDTH + A_HEADS] + fb).reshape(bd, n_q, A_HEADS)
        oa_s = fox_sample_two_pass(page_table, q_s, k_s, v_s, lf_s.transpose(0, 2, 1), cache_kt, cache_vt,
                                   cache_lft, layer=l, group=16, value_group=8).reshape(ms, A_WIDTH)
        ub_s = rest_s[:, :, o_ub:o_ub + B_WIDTH]
        qkvc_s = rest_s[:, :, o_qkvc:o_qkvc + 3 * C_WIDTH]
        pool_full = jnp.concatenate([state_pool[l], ub_s], axis=1)
        conv_full = jnp.concatenate([state_conv[l], qkvc_s], axis=1)
        pre16 = jnp.concatenate([jnp.zeros((bd, 1, B_WIDTH), F32), state_pool[l]], axis=1)
        ob_s = pool_mix(ub_s, pre16, pw, psc, tt=n_q, n_valid=POOL_BUF, out_dtype=F32).reshape(ms, B_WIDTH)
        pad_t = lambda a: jnp.pad(a, ((0, 0), (0, 8 - n_q), (0, 0)))
        pre8 = jnp.concatenate([jnp.zeros((bd, 8 - CONV_BUF, 3 * C_WIDTH), F32), state_conv[l]], axis=1)
        oc_s, ssm_s = gdn_mixer(pad_t(qkvc_s), pad_t(rest_s[:, :, o_ab:o_ab + LANES]),
                                pad_t(rest_s[:, :, o_z:o_z + C_WIDTH]), pre8, state_ssm[l],
                                gdn_conv_w[l], a_log_row, dt_row, norm_w, chunk=8, n_chunks=1, t_valid=n_q,
                                out_dtype=F32)
        ys = merge_ln(hs, oa_s, ob_s, oc_s[:, :n_q].reshape(ms, C_WIDTH), *w_merge, lg[1], lb[1], tm=tm_s)
        ys = ffn_ln(ys, ffn_up[1], ffn_dn[1], lg[2], lb[2], tm=tm_s)
        for lst, a in zip(outs_s, (k_s.reshape(bd, n_q, A_HEADS, A_HEAD_DIM), v_s.reshape(bd, n_q, A_HEADS, A_HEAD_DIM),
                                   lf_s, pool_full[:, -POOL_BUF:], conv_full[:, -CONV_BUF:], ssm_s)):
            lst.append(a)

    k_p, v_p, lf_p, pool_p, conv_p, ssm_p = [jnp.stack(a) for a in outs_p]
    k_s, v_s, lf_s, pool_s, conv_s, ssm_s = [jnp.stack(a) for a in outs_s]
    return (yp.reshape(bp, seq, D_MODEL), ys.reshape(bd, n_q, D_MODEL), k_p, v_p, lf_p, pool_p, conv_p, ssm_p,
            k_s, v_s, lf_s, pool_s, conv_s, ssm_s)
```
